```python
import math
import jax, jax.numpy as jnp
from jax import lax
import numpy as np

D_MODEL = 1024
BATCH = 8
SEQ = 4096
DEPTH = 2

EPS = 1e-6
DA_HEADS = 8
DA_QK_DIM = 64
DA_V_DIM = 2 * DA_QK_DIM
DA_WIDTH = DA_HEADS * DA_V_DIM
Q_BLOCK = 128
ML_HEADS = 4
ML_QK_DIM = 128
ML_V_DIM = 256
ML_WIDTH = ML_HEADS * ML_V_DIM
ML_CHUNK = 128
CONV_WIDTH = 4
SG_GROUPS = 8
SG_CHUNK = 128
SG_WIDTH = 2 * D_MODEL
SG_GROUP_DIM = SG_WIDTH // SG_GROUPS

EVEN_SIZES = (
    DA_HEADS * 2 * DA_QK_DIM,
    DA_HEADS * 2 * DA_QK_DIM,
    DA_WIDTH,
    DA_WIDTH,
    2 * ML_HEADS * ML_QK_DIM,
    ML_WIDTH,
    ML_HEADS,
    ML_HEADS,
    ML_WIDTH,
    ML_WIDTH,
)
EVEN_IN = sum(EVEN_SIZES)
ODD_SIZES = (SG_WIDTH, SG_WIDTH, SG_WIDTH)
ODD_IN = sum(ODD_SIZES)

kernel_name = "hybrid_diffattn_mlstm_gmlp_block"


def _split(t, sizes):
    out, off = [], 0
    for n in sizes:
        out.append(t[..., off:off + n])
        off += n
    return out


def rmsnorm(x, g):
    xf = x.astype(jnp.float32)
    y = xf * lax.rsqrt(jnp.mean(xf * xf, axis=-1, keepdims=True) + EPS)
    return (y * g.astype(jnp.float32)).astype(x.dtype)


def causal_dwconv(x, w, b):
    S = x.shape[1]
    xp = jnp.pad(x, ((0, 0), (CONV_WIDTH - 1, 0), (0, 0)))
    return sum(w[j] * xp[:, j:j + S] for j in range(CONV_WIDTH)) + b


def diff_attention(q, k, v, lam):
    B, S = q.shape[:2]
    nb = S // Q_BLOCK
    qb = (q * DA_QK_DIM ** -0.5).reshape(B, nb, Q_BLOCK, DA_HEADS, 2, DA_QK_DIM)
    qb = qb.transpose(1, 0, 3, 4, 2, 5)
    kt = k.transpose(0, 2, 3, 1, 4)
    vt = v.transpose(0, 2, 1, 3)
    k_pos = jnp.arange(S)

    def block(args):
        qi, i = args
        s = jnp.einsum('bhcqd,bhckd->bhcqk', qi, kt).astype(jnp.float32)
        q_pos = i * Q_BLOCK + jnp.arange(Q_BLOCK)
        s = jnp.where(k_pos[None, :] <= q_pos[:, None], s, -jnp.inf)
        p = jax.nn.softmax(s, axis=-1)
        a = p[:, :, 0] - lam * p[:, :, 1]
        return jnp.einsum('bhqk,bhkv->bhqv', a.astype(vt.dtype), vt)

    o = lax.map(block, (qb, jnp.arange(nb)))
    return o.transpose(1, 0, 3, 2, 4).reshape(B, S, DA_HEADS, DA_V_DIM)


def mlstm_chunkwise(q, k, v, i_pre, f_pre):
    B, S, H, dk = q.shape
    dv = v.shape[-1]
    L = ML_CHUNK
    nc = S // L
    f32 = jnp.float32

    def chunks(t):
        t = t.astype(f32).reshape(B, nc, L, H, *t.shape[3:])
        return jnp.moveaxis(t, 3, 1)

    qc = chunks(q) * dk ** -0.5
    kc = chunks(k)
    vc = chunks(v)
    ic = chunks(i_pre)
    lf = jax.nn.log_sigmoid(chunks(f_pre))
    bcum = jnp.cumsum(lf, axis=-1)
    b_last = bcum[..., -1]
    g = b_last[..., None] - bcum + ic

    def step(carry, inp):
        C, n, m = carry
        bl, gs, ks, vs = inp
        m_new = jnp.maximum(bl + m, jnp.max(gs, axis=-1))
        decay = jnp.exp(bl + m - m_new)
        w = jnp.exp(gs - m_new[..., None])
        C_new = decay[..., None, None] * C + jnp.einsum('bhl,bhlk,bhlv->bhkv', w, ks, vs)
        n_new = decay[..., None] * n + jnp.einsum('bhl,bhlk->bhk', w, ks)
        return (C_new, n_new, m_new), (C, n, m)

    init = (jnp.zeros((B, H, dk, dv), f32), jnp.zeros((B, H, dk), f32), jnp.zeros((B, H), f32))
    xs = (jnp.moveaxis(b_last, 2, 0), jnp.moveaxis(g, 2, 0),
          jnp.moveaxis(kc, 2, 0), jnp.moveaxis(vc, 2, 0))
    _, (C_prev, n_prev, m_prev) = lax.scan(step, init, xs)
    C_prev = jnp.moveaxis(C_prev, 0, 2)
    n_prev = jnp.moveaxis(n_prev, 0, 2)
    m_prev = jnp.moveaxis(m_prev, 0, 2)

    causal = jnp.tril(jnp.ones((L, L), dtype=bool))
    log_d = bcum[..., :, None] - bcum[..., None, :] + ic[..., None, :]
    log_d = jnp.where(causal, log_d, -jnp.inf)
    log_inter = bcum + m_prev[..., None]
    m_t = jnp.maximum(log_inter, jnp.max(log_d, axis=-1))
    dmat = jnp.exp(log_d - m_t[..., None])
    inter_w = jnp.exp(log_inter - m_t)
    qk = jnp.einsum('bhctd,bhcsd->bhcts', qc, kc) * dmat
    num = (jnp.einsum('bhcts,bhcsv->bhctv', qk, vc)
           + inter_w[..., None] * jnp.einsum('bhctd,bhcdv->bhctv', qc, C_prev))
    den = jnp.sum(qk, axis=-1) + inter_w * jnp.einsum('bhctd,bhcd->bhct', qc, n_prev)
    h = num / jnp.maximum(jnp.abs(den), jnp.exp(-m_t))[..., None]
    h = jnp.moveaxis(h, 1, 3).reshape(B, S, H, dv)
    return h.astype(q.dtype)


def even_mixer(h, layer, w_in, b_igate, b_fgate, conv_w, conv_b,
               lambda_q1, lambda_k1, lambda_q2, lambda_k2, da_head_g, ml_head_g, w_out):
    B, S, _ = h.shape
    p = jnp.einsum('bsd,de->bse', h, w_in)
    da_q, da_k, da_v, da_z, ml_qk, ml_v, ml_i, ml_f, ml_o, ml_z = _split(p, EVEN_SIZES)

    f32 = jnp.float32
    lam_init = 0.8 - 0.6 * math.exp(-0.3 * layer)
    lam = (jnp.exp(jnp.dot(lambda_q1.astype(f32), lambda_k1.astype(f32)))
           - jnp.exp(jnp.dot(lambda_q2.astype(f32), lambda_k2.astype(f32))) + lam_init)
    o_a = diff_attention(da_q.reshape(B, S, DA_HEADS, 2, DA_QK_DIM),
                         da_k.reshape(B, S, DA_HEADS, 2, DA_QK_DIM),
                         da_v.reshape(B, S, DA_HEADS, DA_V_DIM), lam)
    o_a = rmsnorm(o_a, da_head_g) * (1.0 - lam_init)
    y_a = o_a.reshape(B, S, DA_WIDTH) * jax.nn.silu(da_z)

    qk = jax.nn.silu(causal_dwconv(ml_qk, conv_w, conv_b))
    ml_q, ml_k = _split(qk, (ML_HEADS * ML_QK_DIM, ML_HEADS * ML_QK_DIM))
    hm = mlstm_chunkwise(ml_q.reshape(B, S, ML_HEADS, ML_QK_DIM),
                         ml_k.reshape(B, S, ML_HEADS, ML_QK_DIM),
                         ml_v.reshape(B, S, ML_HEADS, ML_V_DIM),
                         ml_i + b_igate, ml_f + b_fgate)
    hm = hm * jax.nn.sigmoid(ml_o).reshape(B, S, ML_HEADS, ML_V_DIM)
    hm = rmsnorm(hm, ml_head_g)
    y_b = hm.reshape(B, S, ML_WIDTH) * jax.nn.silu(ml_z)

    y = jnp.concatenate([y_a, y_b], axis=-1)
    return jnp.einsum('bse,ed->bsd', y, w_out)


def odd_mixer(h, sg_norm_g, w_in, w_spatial, b_spatial, w_out):
    B, S, _ = h.shape
    nc = S // SG_CHUNK
    p = jnp.einsum('bsd,de->bse', h, w_in)
    u, v, z = _split(p, ODD_SIZES)
    u = jax.nn.gelu(u)
    v = rmsnorm(jax.nn.gelu(v), sg_norm_g)
    vg = v.reshape(B, nc, SG_CHUNK, SG_GROUPS, SG_GROUP_DIM)
    causal = jnp.tril(jnp.ones((SG_CHUNK, SG_CHUNK), dtype=w_spatial.dtype))
    wm = w_spatial * causal
    vs = jnp.einsum('gts,bcsgd->bctgd', wm, vg) + b_spatial.T[None, None, :, :, None]
    y = u * vs.reshape(B, S, SG_WIDTH) * jax.nn.silu(z)
    return jnp.einsum('bse,ed->bsd', y, w_out)


def setup_inputs(seed: int = 0) -> dict:
    key = jax.random.key(seed)
    ks = jax.random.split(key, 24)
    f32 = jnp.float32

    def nrm(k, shape, scale):
        return jax.random.normal(k, shape, f32) * scale

    def gain(k, n):
        return 1.0 + 0.05 * jax.random.normal(k, (n,), f32)

    return {
        "x": nrm(ks[0], (BATCH, SEQ, D_MODEL), 1.0),
        "l0_pre_g": gain(ks[1], D_MODEL),
        "l0_w_in": nrm(ks[2], (D_MODEL, EVEN_IN), D_MODEL ** -0.5),
        "l0_b_igate": nrm(ks[3], (ML_HEADS,), 0.1),
        "l0_b_fgate": 3.0 + nrm(ks[4], (ML_HEADS,), 0.5),
        "l0_conv_w": nrm(ks[5], (CONV_WIDTH, 2 * ML_HEADS * ML_QK_DIM), CONV_WIDTH ** -0.5),
        "l0_conv_b": nrm(ks[6], (2 * ML_HEADS * ML_QK_DIM,), 0.02),
        "l0_lambda_q1": nrm(ks[7], (DA_QK_DIM,), 0.1),
        "l0_lambda_k1": nrm(ks[8], (DA_QK_DIM,), 0.1),
        "l0_lambda_q2": nrm(ks[9], (DA_QK_DIM,), 0.1),
        "l0_lambda_k2": nrm(ks[10], (DA_QK_DIM,), 0.1),
        "l0_da_head_g": gain(ks[11], DA_V_DIM),
        "l0_ml_head_g": gain(ks[12], ML_V_DIM),
        "l0_w_out": nrm(ks[13], (DA_WIDTH + ML_WIDTH, D_MODEL), (DA_WIDTH + ML_WIDTH) ** -0.5),
        "l0_post_g": gain(ks[14], D_MODEL),
        "l1_pre_g": gain(ks[15], D_MODEL),
        "l1_w_in": nrm(ks[16], (D_MODEL, ODD_IN), D_MODEL ** -0.5),
        "l1_sg_norm_g": gain(ks[17], SG_WIDTH),
        "l1_w_spatial": nrm(ks[18], (SG_GROUPS, SG_CHUNK, SG_CHUNK), SG_CHUNK ** -0.5),
        "l1_b_spatial": 1.0 + nrm(ks[19], (SG_GROUPS, SG_CHUNK), 0.1),
        "l1_w_out": nrm(ks[20], (SG_WIDTH, D_MODEL), SG_WIDTH ** -0.5),
        "l1_post_g": gain(ks[21], D_MODEL),
    }


def reference(x, l0_pre_g, l0_w_in, l0_b_igate, l0_b_fgate, l0_conv_w, l0_conv_b,
              l0_lambda_q1, l0_lambda_k1, l0_lambda_q2, l0_lambda_k2, l0_da_head_g,
              l0_ml_head_g, l0_w_out, l0_post_g,
              l1_pre_g, l1_w_in, l1_sg_norm_g, l1_w_spatial, l1_b_spatial, l1_w_out, l1_post_g):
    layers = [
        (l0_pre_g, l0_post_g, (l0_w_in, l0_b_igate, l0_b_fgate, l0_conv_w, l0_conv_b,
                               l0_lambda_q1, l0_lambda_k1, l0_lambda_q2, l0_lambda_k2,
                               l0_da_head_g, l0_ml_head_g, l0_w_out)),
        (l1_pre_g, l1_post_g, (l1_sg_norm_g, l1_w_in, l1_w_spatial, l1_b_spatial, l1_w_out)),
    ]
    h = x
    for l in range(DEPTH):
        pre_g, post_g, params = layers[l]
        hn = rmsnorm(h, pre_g)
        if l % 2 == 0:
            y = even_mixer(hn, l, *params)
        else:
            y = odd_mixer(hn, *params)
        h = h + rmsnorm(y, post_g)
    return h
```

```python
import functools
import math

import jax
import jax.numpy as jnp
from jax import lax
from jax.experimental import pallas as pl
from jax.experimental.pallas import tpu as pltpu

F32 = jnp.float32
BF16 = jnp.bfloat16

D_MODEL = 1024
EPS = 1e-6
DA_HEADS = 8
DA_QK_DIM = 64
DA_V_DIM = 128
DA_WIDTH = DA_HEADS * DA_V_DIM
ML_HEADS = 4
ML_QK_DIM = 128
ML_V_DIM = 256
ML_WIDTH = ML_HEADS * ML_V_DIM
ML_CHUNK = 128
CONV_WIDTH = 4
SG_GROUPS = 8
SG_CHUNK = 128
SG_WIDTH = 2 * D_MODEL
SG_GROUP_DIM = SG_WIDTH // SG_GROUPS

LANES = 128
SUM_ROWS = 16
VMEM_LIMIT = 56 * 1024 * 1024
NEG_BIG = -1e30
LOG2E = 1.4426950408889634

_OFF_AQ, _OFF_AK, _OFF_AV, _OFF_AZ = 0, 1024, 2048, 3072
_OFF_BQK, _OFF_BV, _OFF_BI, _OFF_BF, _OFF_BO, _OFF_BZ = 4096, 5120, 6144, 6148, 6152, 7176
L0_MAIN = 8192


def _sigmoid(x):
    return 1.0 / (1.0 + jnp.exp(-x))


def _silu(x):
    return x * _sigmoid(x)


def _gelu_tanh(x):
    c = math.sqrt(2.0 / math.pi)
    return 0.5 * x * (1.0 + jnp.tanh(c * (x + 0.044715 * (x * x * x))))


def _rms_scale(x):
    return lax.rsqrt(jnp.mean(x * x, axis=-1, keepdims=True) + EPS)


def _dot(a, b):
    return jnp.dot(a, b, preferred_element_type=F32)


def _l0_in_proj_kernel(x_ref, g_ref, w_ref, wg_ref, bg_ref, p_ref, gt_ref, hn_ref, *, q_scale):
    j = pl.program_id(1)

    @pl.when(j == 0)
    def _():
        x = x_ref[...]
        hn = (x * _rms_scale(x) * g_ref[...]).astype(BF16)
        hn_ref[...] = hn
        gates = _dot(hn, wg_ref[...]) + bg_ref[...]
        gt_ref[...] = gates.T[0:8, :]

    acc = _dot(hn_ref[...], w_ref[...])
    scale = jnp.where(j == 0, q_scale, 1.0).astype(F32)
    p_ref[...] = (acc * scale).astype(BF16)


def _l0_in_proj(x2, pre_g, w_main, w_gate, b_gate, *, tm=1024, tn=1024):
    m = x2.shape[0]
    q_scale = DA_QK_DIM ** -0.5 * LOG2E
    return pl.pallas_call(
        functools.partial(_l0_in_proj_kernel, q_scale=q_scale),
        grid=(m // tm, L0_MAIN // tn),
        in_specs=[
            pl.BlockSpec((tm, D_MODEL), lambda i, j: (i, 0)),
            pl.BlockSpec((1, D_MODEL), lambda i, j: (0, 0)),
            pl.BlockSpec((D_MODEL, tn), lambda i, j: (0, j)),
            pl.BlockSpec((D_MODEL, LANES), lambda i, j: (0, 0)),
            pl.BlockSpec((1, LANES), lambda i, j: (0, 0)),
        ],
        out_specs=[
            pl.BlockSpec((tm, tn), lambda i, j: (i, j)),
            pl.BlockSpec((8, tm), lambda i, j: (0, i)),
        ],
        out_shape=[
            jax.ShapeDtypeStruct((m, L0_MAIN), BF16),
            jax.ShapeDtypeStruct((8, m), F32),
        ],
        scratch_shapes=[pltpu.VMEM((tm, D_MODEL), BF16)],
        compiler_params=pltpu.CompilerParams(
            dimension_semantics=("arbitrary", "arbitrary"), vmem_limit_bytes=VMEM_LIMIT),
        name="l0_in_proj",
    )(x2, pre_g, w_main, w_gate, b_gate)


def _diff_attn_kernel(lam_ref, q_ref, k_ref, v_ref, z_ref, g_ref, o_ref,
                      vt_ref, acc_ref, *, tq, lam_init):
    qi = pl.program_id(2)
    nk = vt_ref.shape[0]
    tk = tq

    @pl.when(qi == 0)
    def _():
        ones_row = (lax.broadcasted_iota(jnp.int32, (SUM_ROWS, tk), 0) == 0).astype(F32)
        for kb in range(nk):
            vt = v_ref[0, kb * tk:(kb + 1) * tk, :].astype(F32).T
            vt_ref[kb] = jnp.concatenate([vt, ones_row], axis=0).astype(BF16)

    lam_v = lam_ref[...]
    lam = (jnp.exp(jnp.sum(lam_v[0:1] * lam_v[1:2], axis=-1, keepdims=True))
           - jnp.exp(jnp.sum(lam_v[2:3] * lam_v[3:4], axis=-1, keepdims=True)) + lam_init)

    qt = q_ref[0].astype(F32).T
    row = lax.broadcasted_iota(jnp.int32, (LANES, tq), 0)
    first = row < DA_QK_DIM
    qbd = jnp.concatenate([jnp.where(first, qt, 0.0), jnp.where(first, 0.0, qt)], axis=1).astype(BF16)

    acc_ref[...] = jnp.zeros_like(acc_ref)

    def step(kb, m, masked):
        k = k_ref[0, pl.ds(pl.multiple_of(kb * tk, tk), tk), :]
        s = _dot(k, qbd)
        if masked:
            key = lax.broadcasted_iota(jnp.int32, (tk, 2 * tq), 0)
            qry = lax.broadcasted_iota(jnp.int32, (tk, 2 * tq), 1) & (tq - 1)
            s = jnp.where(key <= qry, s, NEG_BIG)
        m_new = jnp.maximum(m, jnp.max(s, axis=0, keepdims=True))
        alpha = jnp.exp2(m - m_new)
        p = jnp.exp2(s - m_new).astype(BF16)
        acc_ref[...] = alpha * acc_ref[...] + _dot(vt_ref[kb], p)
        return m_new

    m0 = jnp.full((1, 2 * tq), NEG_BIG, F32)
    m = lax.fori_loop(0, qi, lambda kb, m: step(kb, m, False), m0)
    step(qi, m, True)

    acc = acc_ref[...]
    o_t = acc[0:LANES, :] / acc[LANES:LANES + 1, :]
    o = (o_t[:, 0:tq] - lam * o_t[:, tq:2 * tq]).T
    o = o * _rms_scale(o) * g_ref[...] * (1.0 - lam_init)
    o_ref[0] = (o * _silu(z_ref[0].astype(F32))).astype(BF16)


def _diff_attn(p3, lam_vecs, head_g, *, tq=256):
    b, s, _ = p3.shape
    nq = s // tq
    hq, hk, hv, hz = (off // LANES for off in (_OFF_AQ, _OFF_AK, _OFF_AV, _OFF_AZ))
    lam_init = 0.8 - 0.6 * math.exp(-0.3 * 0)
    return pl.pallas_call(
        functools.partial(_diff_attn_kernel, tq=tq, lam_init=lam_init),
        grid=(b, DA_HEADS, nq),
        in_specs=[
            pl.BlockSpec((4, DA_QK_DIM), lambda bi, h, qi: (0, 0)),
            pl.BlockSpec((1, tq, LANES), lambda bi, h, qi: (bi, qi, hq + h)),
            pl.BlockSpec((1, s, LANES), lambda bi, h, qi: (bi, 0, hk + h)),
            pl.BlockSpec((1, s, LANES), lambda bi, h, qi: (bi, 0, hv + h)),
            pl.BlockSpec((1, tq, LANES), lambda bi, h, qi: (bi, qi, hz + h)),
            pl.BlockSpec((1, DA_V_DIM), lambda bi, h, qi: (0, 0)),
        ],
        out_specs=pl.BlockSpec((1, tq, LANES), lambda bi, h, qi: (bi, qi, h)),
        out_shape=jax.ShapeDtypeStruct((b, s, DA_WIDTH), BF16),
        scratch_shapes=[
            pltpu.VMEM((s // tq, LANES + SUM_ROWS, tq), BF16),
            pltpu.VMEM((LANES + SUM_ROWS, 2 * tq), F32),
        ],
        compiler_params=pltpu.CompilerParams(
            dimension_semantics=("arbitrary", "arbitrary", "arbitrary"), vmem_limit_bytes=VMEM_LIMIT),
        name="diff_attn",
    )(lam_vecs, p3, p3, p3, p3, head_g)


def _mlstm_kernel(qp_ref, kp_ref, v_ref, o_ref, z_ref, gt_ref, cwq_ref, cwk_ref, cbq_ref, cbk_ref,
                  hg_ref, y_ref, xq_ref, xk_ref, ic_ref, bc_ref, c_ref):
    h = pl.program_id(1)
    s = qp_ref.shape[1]
    L = ML_CHUNK
    nc = s // L
    pad = 8

    zeros = jnp.zeros((pad, ML_QK_DIM), F32)
    xq_ref[0:pad, :] = zeros
    xk_ref[0:pad, :] = zeros
    xq_ref[pad:, :] = qp_ref[0].astype(F32)
    xk_ref[pad:, :] = kp_ref[0].astype(F32)

    ic = gt_ref[h]
    fp = gt_ref[ML_HEADS + h]
    lf = jnp.minimum(fp, 0.0) - jnp.log(1.0 + jnp.exp(-jnp.abs(fp)))
    r_i = lax.broadcasted_iota(jnp.int32, (L, L), 0)
    c_i = lax.broadcasted_iota(jnp.int32, (L, L), 1)
    tri_u = (r_i <= c_i).astype(BF16)
    lf_hi = lf.astype(BF16)
    lf_lo = (lf - lf_hi.astype(F32)).astype(BF16)
    ic_ref[...] = ic
    bc_ref[...] = _dot(lf_hi, tri_u) + _dot(lf_lo, tri_u)

    c_ref[...] = jnp.zeros_like(c_ref)
    causal = c_i <= r_i
    ones_col = (lax.broadcasted_iota(jnp.int32, (L, LANES), 1) == 0).astype(BF16)
    cwq = cwq_ref[...]
    cwk = cwk_ref[...]

    def conv(x_ref, r0, w, bias):
        xw = x_ref[pl.ds(r0, L + pad), :]
        acc = bias
        for j in range(CONV_WIDTH):
            lo = pad - (CONV_WIDTH - 1) + j
            acc = acc + w[j:j + 1, :] * xw[lo:lo + L, :]
        return _silu(acc)

    def chunk(c, m_prev):
        r0 = pl.multiple_of(c * L, L)
        q = (conv(xq_ref, r0, cwq, cbq_ref[...]) * ML_QK_DIM ** -0.5).astype(BF16)
        k_t = conv(xk_ref, r0, cwk, cbk_ref[...]).T
        v_ext = jnp.concatenate([v_ref[0, pl.ds(r0, L), :], ones_col], axis=1)

        ic_row = ic_ref[pl.ds(c, 1), :]
        bc_row = bc_ref[pl.ds(c, 1), :]
        bc_col = jnp.broadcast_to(bc_row, (8, L)).T[:, 0:1]
        b_last = bc_row[:, L - 1:L]

        log_d = jnp.where(causal, bc_col - bc_row + ic_row, NEG_BIG)
        log_inter = bc_col + m_prev
        m_t = jnp.maximum(log_inter, jnp.max(log_d, axis=-1, keepdims=True))
        dmat = jnp.exp(log_d - m_t)
        inter_w = jnp.exp(log_inter - m_t)
        qk = (_dot(q, k_t.astype(BF16)) * dmat).astype(BF16)
        nd = _dot(qk, v_ext) + inter_w * _dot(q, c_ref[...].astype(BF16))
        num = nd[:, 0:ML_V_DIM]
        den = nd[:, ML_V_DIM:ML_V_DIM + 1]
        hm = num / jnp.maximum(jnp.abs(den), jnp.exp(-m_t))
        hm = hm * _sigmoid(o_ref[0, pl.ds(r0, L), :].astype(F32))
        hm = hm * _rms_scale(hm) * hg_ref[...]
        y_ref[0, pl.ds(r0, L), :] = (hm * _silu(z_ref[0, pl.ds(r0, L), :].astype(F32))).astype(BF16)

        g_row = b_last - bc_row + ic_row
        m_new = jnp.maximum(b_last + m_prev, jnp.max(g_row, axis=-1, keepdims=True))
        decay = jnp.exp(b_last + m_prev - m_new)
        w_row = jnp.exp(g_row - m_new)
        c_ref[...] = decay * c_ref[...] + _dot((k_t * w_row).astype(BF16), v_ext)
        return m_new

    lax.fori_loop(0, nc, chunk, jnp.zeros((1, 1), F32))


def _mlstm(p3, gates_t, conv_w, conv_b, head_g):
    b, s, _ = p3.shape
    nc = s // ML_CHUNK
    cq = _OFF_BQK // ML_QK_DIM
    ck = cq + ML_HEADS
    cv, co, cz = (off // ML_V_DIM for off in (5120, 6144, 7168))
    seq = lambda width, col0: pl.BlockSpec((1, s, width), lambda bi, h: (bi, 0, col0 + h))
    return pl.pallas_call(
        _mlstm_kernel,
        grid=(b, ML_HEADS),
        in_specs=[
            seq(ML_QK_DIM, cq), seq(ML_QK_DIM, ck), seq(ML_V_DIM, cv), seq(ML_V_DIM, co), seq(ML_V_DIM, cz),
            pl.BlockSpec((2 * ML_HEADS, nc, ML_CHUNK), lambda bi, h: (0, bi, 0)),
            pl.BlockSpec((CONV_WIDTH, ML_QK_DIM), lambda bi, h: (0, h)),
            pl.BlockSpec((CONV_WIDTH, ML_QK_DIM), lambda bi, h: (0, ML_HEADS + h)),
            pl.BlockSpec((1, ML_QK_DIM), lambda bi, h: (0, h)),
            pl.BlockSpec((1, ML_QK_DIM), lambda bi, h: (0, ML_HEADS + h)),
            pl.BlockSpec((1, ML_V_DIM), lambda bi, h: (0, 0)),
        ],
        out_specs=pl.BlockSpec((1, s, ML_V_DIM), lambda bi, h: (bi, 0, h)),
        out_shape=jax.ShapeDtypeStruct((b, s, ML_WIDTH), BF16),
        scratch_shapes=[
            pltpu.VMEM((s + 8, ML_QK_DIM), F32),
            pltpu.VMEM((s + 8, ML_QK_DIM), F32),
            pltpu.VMEM((nc, ML_CHUNK), F32),
            pltpu.VMEM((nc, ML_CHUNK), F32),
            pltpu.VMEM((ML_QK_DIM, ML_V_DIM + LANES), F32),
        ],
        compiler_params=pltpu.CompilerParams(
            dimension_semantics=("arbitrary", "arbitrary"), vmem_limit_bytes=VMEM_LIMIT),
        name="mlstm",
    )(p3, p3, p3, p3, p3, gates_t, conv_w, conv_w, conv_b, conv_b, head_g)


def _l0_out_proj_kernel(ya_ref, yb_ref, x_ref, wa_ref, wb_ref, g_ref, h_ref):
    y = _dot(ya_ref[...], wa_ref[...]) + _dot(yb_ref[...], wb_ref[...])
    h_ref[...] = x_ref[...] + y * _rms_scale(y) * g_ref[...]


def _l0_out_proj(ya, yb, x2, wa, wb, post_g, *, tm=512):
    m = x2.shape[0]
    row = lambda width: pl.BlockSpec((tm, width), lambda i: (i, 0))
    const = lambda shape: pl.BlockSpec(shape, lambda i: (0, 0))
    return pl.pallas_call(
        _l0_out_proj_kernel,
        grid=(m // tm,),
        in_specs=[row(DA_WIDTH), row(ML_WIDTH), row(D_MODEL),
                  const((DA_WIDTH, D_MODEL)), const((ML_WIDTH, D_MODEL)), const((1, D_MODEL))],
        out_specs=row(D_MODEL),
        out_shape=jax.ShapeDtypeStruct((m, D_MODEL), F32),
        compiler_params=pltpu.CompilerParams(
            dimension_semantics=("arbitrary",), vmem_limit_bytes=VMEM_LIMIT),
        name="l0_out_proj",
    )(ya, yb, x2, wa, wb, post_g)


def _l1_gmlp_kernel(h_ref, pre_g_ref, wu_ref, wv_ref, wz_ref, sg_g_ref, wsp_ref, bsp_ref,
                    wo_ref, post_g_ref, o_ref, vn_ref, y_ref, *, tm):
    hres = h_ref[...]
    hn = (hres * _rms_scale(hres) * pre_g_ref[...]).astype(BF16)

    v = _gelu_tanh(_dot(hn, wv_ref[...]))
    vn_ref[...] = (v * _rms_scale(v) * sg_g_ref[...]).astype(BF16)

    r_i = lax.broadcasted_iota(jnp.int32, (SG_CHUNK, SG_CHUNK), 0)
    c_i = lax.broadcasted_iota(jnp.int32, (SG_CHUNK, SG_CHUNK), 1)
    causal = c_i <= r_i
    for g in range(SG_GROUPS):
        cols = slice(g * SG_GROUP_DIM, (g + 1) * SG_GROUP_DIM)
        u = _gelu_tanh(_dot(hn, wu_ref[:, cols]))
        gate = _silu(_dot(hn, wz_ref[:, cols]))
        wm = jnp.where(causal, wsp_ref[g], 0.0).astype(BF16)
        bias = bsp_ref[g]
        for c in range(tm // SG_CHUNK):
            rows = slice(c * SG_CHUNK, (c + 1) * SG_CHUNK)
            vs = _dot(wm, vn_ref[rows, cols]) + bias
            y_ref[rows, cols] = (u[rows] * vs * gate[rows]).astype(BF16)

    out = _dot(y_ref[...], wo_ref[...])
    o_ref[...] = hres + out * _rms_scale(out) * post_g_ref[...]


def _l1_gmlp(h1, pre_g, wu, wv, wz, sg_g, w_spatial, b_spatial, wo, post_g, *, tm=256):
    m = h1.shape[0]
    row = pl.BlockSpec((tm, D_MODEL), lambda i: (i, 0))
    const2 = lambda shape: pl.BlockSpec(shape, lambda i: (0, 0), pipeline_mode=pl.Buffered(1))
    const3 = lambda shape: pl.BlockSpec(shape, lambda i: (0, 0, 0), pipeline_mode=pl.Buffered(1))
    return pl.pallas_call(
        functools.partial(_l1_gmlp_kernel, tm=tm),
        grid=(m // tm,),
        in_specs=[
            row, const2((1, D_MODEL)),
            const2((D_MODEL, SG_WIDTH)), const2((D_MODEL, SG_WIDTH)), const2((D_MODEL, SG_WIDTH)),
            const2((1, SG_WIDTH)),
            const3((SG_GROUPS, SG_CHUNK, SG_CHUNK)), const3((SG_GROUPS, SG_CHUNK, 1)),
            const2((SG_WIDTH, D_MODEL)), const2((1, D_MODEL)),
        ],
        out_specs=row,
        out_shape=jax.ShapeDtypeStruct((m, D_MODEL), F32),
        scratch_shapes=[pltpu.VMEM((tm, SG_WIDTH), BF16), pltpu.VMEM((tm, SG_WIDTH), BF16)],
        compiler_params=pltpu.CompilerParams(
            dimension_semantics=("arbitrary",), vmem_limit_bytes=VMEM_LIMIT),
        name="l1_gmlp",
    )(h1, pre_g, wu, wv, wz, sg_g, w_spatial, b_spatial, wo, post_g)


def kernel(x, l0_pre_g, l0_w_in, l0_b_igate, l0_b_fgate, l0_conv_w, l0_conv_b, l0_lambda_q1, l0_lambda_k1, l0_lambda_q2, l0_lambda_k2, l0_da_head_g, l0_ml_head_g, l0_w_out, l0_post_g, l1_pre_g, l1_w_in, l1_sg_norm_g, l1_w_spatial, l1_b_spatial, l1_w_out, l1_post_g):
    b, s, d = x.shape
    m = b * s
    x2 = x.reshape(m, d)
    row = lambda v: v.reshape(1, -1).astype(F32)

    w_main = jnp.concatenate([l0_w_in[:, :_OFF_BI], l0_w_in[:, _OFF_BO:]], axis=1).astype(BF16)
    w_gate = jnp.pad(l0_w_in[:, _OFF_BI:_OFF_BO], ((0, 0), (0, LANES - 2 * ML_HEADS))).astype(BF16)
    b_gate = jnp.pad(jnp.concatenate([l0_b_igate, l0_b_fgate]), (0, LANES - 2 * ML_HEADS)).reshape(1, LANES)
    lam_vecs = jnp.stack([l0_lambda_q1, l0_lambda_k1, l0_lambda_q2, l0_lambda_k2]).astype(F32)

    p, gates_t = _l0_in_proj(x2, row(l0_pre_g), w_main, w_gate, b_gate.astype(F32))
    p3 = p.reshape(b, s, L0_MAIN)
    gates_t = gates_t.reshape(2 * ML_HEADS, m // ML_CHUNK, ML_CHUNK)

    y_a = _diff_attn(p3, lam_vecs, row(l0_da_head_g))
    y_b = _mlstm(p3, gates_t, l0_conv_w.astype(F32), row(l0_conv_b), row(l0_ml_head_g))

    w_out0 = l0_w_out.astype(BF16)
    h1 = _l0_out_proj(y_a.reshape(m, DA_WIDTH), y_b.reshape(m, ML_WIDTH), x2,
                      w_out0[:DA_WIDTH], w_out0[DA_WIDTH:], row(l0_post_g))

    w_in1 = l1_w_in.astype(BF16)
    h2 = _l1_gmlp(h1, row(l1_pre_g),
                  w_in1[:, :SG_WIDTH], w_in1[:, SG_WIDTH:2 * SG_WIDTH], w_in1[:, 2 * SG_WIDTH:],
                  row(l1_sg_norm_g), l1_w_spatial.astype(F32),
                  l1_b_spatial.astype(F32).reshape(SG_GROUPS, SG_CHUNK, 1),
                  l1_w_out.astype(BF16), row(l1_post_g))
    return h2.reshape(b, s, d)
```

```python
import functools
import math

import jax
import jax.numpy as jnp
from jax import lax
from jax.experimental import pallas as pl
from jax.experimental.pallas import tpu as pltpu

F32 = jnp.float32
BF16 = jnp.bfloat16

D_MODEL = 1024
EPS = 1e-6
DA_HEADS = 8
DA_QK_DIM = 64
DA_V_DIM = 128
DA_WIDTH = DA_HEADS * DA_V_DIM
ML_HEADS = 4
ML_QK_DIM = 128
ML_V_DIM = 256
ML_WIDTH = ML_HEADS * ML_V_DIM
ML_CHUNK = 128
CONV_WIDTH = 4
SG_GROUPS = 8
SG_CHUNK = 128
SG_WIDTH = 2 * D_MODEL
SG_GROUP_DIM = SG_WIDTH // SG_GROUPS

LANES = 128
SUM_ROWS = 16
VMEM_LIMIT = 56 * 1024 * 1024
NEG_BIG = -1e30
LOG2E = 1.4426950408889634

_OFF_AQ, _OFF_AK, _OFF_AV, _OFF_AZ = 0, 1024, 2048, 3072
_OFF_BQK, _OFF_BV, _OFF_BI, _OFF_BF, _OFF_BO, _OFF_BZ = 4096, 5120, 6144, 6148, 6152, 7176
L0_MAIN = 8192


def _sigmoid(x):
    return 1.0 / (1.0 + jnp.exp(-x))


def _silu(x):
    return x * _sigmoid(x)


def _gelu_tanh(x):
    c = math.sqrt(2.0 / math.pi)
    return 0.5 * x * (1.0 + jnp.tanh(c * (x + 0.044715 * (x * x * x))))


def _rms_scale(x):
    return lax.rsqrt(jnp.mean(x * x, axis=-1, keepdims=True) + EPS)


def _dot(a, b):
    return jnp.dot(a, b, preferred_element_type=F32)


def _l0_in_proj_kernel(x_ref, g_ref, w_ref, wg_ref, bg_ref, p_ref, gt_ref, hn_ref, *, q_scale):
    j = pl.program_id(1)

    @pl.when(j == 0)
    def _():
        x = x_ref[...]
        hn = (x * _rms_scale(x) * g_ref[...]).astype(BF16)
        hn_ref[...] = hn
        gates = _dot(hn, wg_ref[...]) + bg_ref[...]
        gt_ref[...] = gates.T[0:8, :]

    acc = _dot(hn_ref[...], w_ref[...])
    scale = jnp.where(j == 0, q_scale, 1.0).astype(F32)
    p_ref[...] = (acc * scale).astype(BF16)


def _l0_in_proj(x2, pre_g, w_main, w_gate, b_gate, *, tm=1024, tn=1024):
    m = x2.shape[0]
    q_scale = DA_QK_DIM ** -0.5 * LOG2E
    return pl.pallas_call(
        functools.partial(_l0_in_proj_kernel, q_scale=q_scale),
        grid=(m // tm, L0_MAIN // tn),
        in_specs=[
            pl.BlockSpec((tm, D_MODEL), lambda i, j: (i, 0)),
            pl.BlockSpec((1, D_MODEL), lambda i, j: (0, 0)),
            pl.BlockSpec((D_MODEL, tn), lambda i, j: (0, j)),
            pl.BlockSpec((D_MODEL, LANES), lambda i, j: (0, 0)),
            pl.BlockSpec((1, LANES), lambda i, j: (0, 0)),
        ],
        out_specs=[
            pl.BlockSpec((tm, tn), lambda i, j: (i, j)),
            pl.BlockSpec((8, tm), lambda i, j: (0, i)),
        ],
        out_shape=[
            jax.ShapeDtypeStruct((m, L0_MAIN), BF16),
            jax.ShapeDtypeStruct((8, m), F32),
        ],
        scratch_shapes=[pltpu.VMEM((tm, D_MODEL), BF16)],
        compiler_params=pltpu.CompilerParams(
            dimension_semantics=("arbitrary", "arbitrary"), vmem_limit_bytes=VMEM_LIMIT),
        name="l0_in_proj",
    )(x2, pre_g, w_main, w_gate, b_gate)


def _diff_attn_kernel(vq_ref, vk_ref, lam_ref, q_ref, k_ref, v_ref, z_ref, g_ref, o_ref,
                      vt_ref, qbd_ref, acc_ref, m_ref, s0_ref, s1_ref, *, tq, lam_init):
    nq = vt_ref.shape[0]
    tk = tq
    n_off = vq_ref.shape[0]

    lam_v = lam_ref[...]
    lam = (jnp.exp(jnp.sum(lam_v[0:1] * lam_v[1:2], axis=-1, keepdims=True))
           - jnp.exp(jnp.sum(lam_v[2:3] * lam_v[3:4], axis=-1, keepdims=True)) + lam_init)

    ones_row = (lax.broadcasted_iota(jnp.int32, (SUM_ROWS, tk), 0) == 0).astype(F32)
    first = lax.broadcasted_iota(jnp.int32, (LANES, tq), 0) < DA_QK_DIM
    for i in range(nq):
        rows = slice(i * tq, (i + 1) * tq)
        vt = v_ref[0, rows, :].astype(F32).T
        vt_ref[i] = jnp.concatenate([vt, ones_row], axis=0).astype(BF16)
        qt = q_ref[0, rows, :].astype(F32).T
        qbd_ref[i] = jnp.concatenate(
            [jnp.where(first, qt, 0.0), jnp.where(first, 0.0, qt)], axis=1).astype(BF16)
    acc_ref[...] = jnp.zeros_like(acc_ref)
    m_ref[...] = jnp.full(m_ref.shape, NEG_BIG, F32)

    key = lax.broadcasted_iota(jnp.int32, (tk, 2 * tq), 0)
    qry = lax.broadcasted_iota(jnp.int32, (tk, 2 * tq), 1) & (tq - 1)

    def scores(qi, kb, masked):
        k = k_ref[0, pl.ds(pl.multiple_of(kb * tk, tk), tk), :]
        s = _dot(k, qbd_ref[qi])
        return jnp.where(key <= qry, s, NEG_BIG) if masked else s

    def softmax_pv(s_ref, qi, kb):
        s = s_ref[...]
        m = m_ref[qi]
        m_new = jnp.maximum(m, jnp.max(s, axis=0, keepdims=True))
        alpha = jnp.exp2(m - m_new)
        p = jnp.exp2(s - m_new).astype(BF16)
        acc_ref[qi] = alpha * acc_ref[qi] + _dot(vt_ref[kb], p)
        m_ref[qi] = m_new

    def visit_loop(n, block_of, masked):
        s0_ref[...] = scores(*block_of(0), masked)

        def body(t, carry):
            s1_ref[...] = scores(*block_of(2 * t + 1), masked)
            softmax_pv(s0_ref, *block_of(2 * t))
            s0_ref[...] = scores(*block_of(jnp.minimum(2 * t + 2, n - 1)), masked)
            softmax_pv(s1_ref, *block_of(2 * t + 1))
            return carry

        lax.fori_loop(0, n // 2, body, 0)

    visit_loop(nq, lambda i: (i, i), True)
    visit_loop(n_off, lambda i: (vq_ref[i], vk_ref[i]), False)

    for i in range(nq):
        rows = slice(i * tq, (i + 1) * tq)
        acc = acc_ref[i]
        o_t = acc[0:LANES, :] / acc[LANES:LANES + 1, :]
        o = (o_t[:, 0:tq] - lam * o_t[:, tq:2 * tq]).T
        o = o * _rms_scale(o) * g_ref[...] * (1.0 - lam_init)
        o_ref[0, rows, :] = (o * _silu(z_ref[0, rows, :].astype(F32))).astype(BF16)


def _diff_attn(p3, lam_vecs, head_g, *, tq=512):
    b, s, _ = p3.shape
    nq = s // tq
    hq, hk, hv, hz = (off // LANES for off in (_OFF_AQ, _OFF_AK, _OFF_AV, _OFF_AZ))
    lam_init = 0.8 - 0.6 * math.exp(-0.3 * 0)
    below = [(qi, kb) for qi in range(nq) for kb in range(qi)]
    assert nq % 2 == 0 and len(below) % 2 == 0
    visit_q = jnp.asarray([qi for qi, _ in below], jnp.int32)
    visit_k = jnp.asarray([kb for _, kb in below], jnp.int32)
    seq = lambda col0: pl.BlockSpec((1, s, LANES), lambda bi, h, vq, vk: (bi, 0, col0 + h))
    return pl.pallas_call(
        functools.partial(_diff_attn_kernel, tq=tq, lam_init=lam_init),
        grid_spec=pltpu.PrefetchScalarGridSpec(
            num_scalar_prefetch=2,
            grid=(b, DA_HEADS),
            in_specs=[
                pl.BlockSpec((4, DA_QK_DIM), lambda bi, h, vq, vk: (0, 0)),
                seq(hq), seq(hk), seq(hv), seq(hz),
                pl.BlockSpec((1, DA_V_DIM), lambda bi, h, vq, vk: (0, 0)),
            ],
            out_specs=pl.BlockSpec((1, s, LANES), lambda bi, h, vq, vk: (bi, 0, h)),
            scratch_shapes=[
                pltpu.VMEM((nq, LANES + SUM_ROWS, tq), BF16),
                pltpu.VMEM((nq, LANES, 2 * tq), BF16),
                pltpu.VMEM((nq, LANES + SUM_ROWS, 2 * tq), F32),
                pltpu.VMEM((nq, 1, 2 * tq), F32),
                pltpu.VMEM((tq, 2 * tq), F32),
                pltpu.VMEM((tq, 2 * tq), F32),
            ],
        ),
        out_shape=jax.ShapeDtypeStruct((b, s, DA_WIDTH), BF16),
        compiler_params=pltpu.CompilerParams(
            dimension_semantics=("arbitrary", "arbitrary"), vmem_limit_bytes=VMEM_LIMIT),
        name="diff_attn",
    )(visit_q, visit_k, lam_vecs, p3, p3, p3, p3, head_g)


def _mlstm_kernel(qp_ref, kp_ref, v_ref, o_ref, z_ref, gt_ref, cwq_ref, cwk_ref, cbq_ref, cbk_ref,
                  hg_ref, y_ref, xq_ref, xk_ref, ic_ref, bc_ref, c_ref):
    h = pl.program_id(1)
    s = qp_ref.shape[1]
    L = ML_CHUNK
    nc = s // L
    pad = 8

    zeros = jnp.zeros((pad, ML_QK_DIM), F32)
    xq_ref[0:pad, :] = zeros
    xk_ref[0:pad, :] = zeros
    xq_ref[pad:, :] = qp_ref[0].astype(F32)
    xk_ref[pad:, :] = kp_ref[0].astype(F32)

    ic = gt_ref[h]
    fp = gt_ref[ML_HEADS + h]
    lf = jnp.minimum(fp, 0.0) - jnp.log(1.0 + jnp.exp(-jnp.abs(fp)))
    r_i = lax.broadcasted_iota(jnp.int32, (L, L), 0)
    c_i = lax.broadcasted_iota(jnp.int32, (L, L), 1)
    tri_u = (r_i <= c_i).astype(BF16)
    lf_hi = lf.astype(BF16)
    lf_lo = (lf - lf_hi.astype(F32)).astype(BF16)
    ic_ref[...] = ic
    bc_ref[...] = _dot(lf_hi, tri_u) + _dot(lf_lo, tri_u)

    c_ref[...] = jnp.zeros_like(c_ref)
    causal = c_i <= r_i
    ones_col = (lax.broadcasted_iota(jnp.int32, (L, LANES), 1) == 0).astype(BF16)
    cwq = cwq_ref[...]
    cwk = cwk_ref[...]

    def conv(x_ref, r0, w, bias):
        xw = x_ref[pl.ds(r0, L + pad), :]
        acc = bias
        for j in range(CONV_WIDTH):
            lo = pad - (CONV_WIDTH - 1) + j
            acc = acc + w[j:j + 1, :] * xw[lo:lo + L, :]
        return _silu(acc)

    def chunk(c, m_prev):
        r0 = pl.multiple_of(c * L, L)
        q = (conv(xq_ref, r0, cwq, cbq_ref[...]) * ML_QK_DIM ** -0.5).astype(BF16)
        k_t = conv(xk_ref, r0, cwk, cbk_ref[...]).T
        v_ext = jnp.concatenate([v_ref[0, pl.ds(r0, L), :], ones_col], axis=1)

        ic_row = ic_ref[pl.ds(c, 1), :]
        bc_row = bc_ref[pl.ds(c, 1), :]
        bc_col = jnp.broadcast_to(bc_row, (8, L)).T[:, 0:1]
        b_last = bc_row[:, L - 1:L]

        log_d = jnp.where(causal, bc_col - bc_row + ic_row, NEG_BIG)
        log_inter = bc_col + m_prev
        m_t = jnp.maximum(log_inter, jnp.max(log_d, axis=-1, keepdims=True))
        dmat = jnp.exp(log_d - m_t)
        inter_w = jnp.exp(log_inter - m_t)
        qk = (_dot(q, k_t.astype(BF16)) * dmat).astype(BF16)
        nd = _dot(qk, v_ext) + inter_w * _dot(q, c_ref[...].astype(BF16))
        num = nd[:, 0:ML_V_DIM]
        den = nd[:, ML_V_DIM:ML_V_DIM + 1]
        hm = num / jnp.maximum(jnp.abs(den), jnp.exp(-m_t))
        hm = hm * _sigmoid(o_ref[0, pl.ds(r0, L), :].astype(F32))
        hm = hm * _rms_scale(hm) * hg_ref[...]
        y_ref[0, pl.ds(r0, L), :] = (hm * _silu(z_ref[0, pl.ds(r0, L), :].astype(F32))).astype(BF16)

        g_row = b_last - bc_row + ic_row
        m_new = jnp.maximum(b_last + m_prev, jnp.max(g_row, axis=-1, keepdims=True))
        decay = jnp.exp(b_last + m_prev - m_new)
        w_row = jnp.exp(g_row - m_new)
        c_ref[...] = decay * c_ref[...] + _dot((k_t * w_row).astype(BF16), v_ext)
        return m_new

    lax.fori_loop(0, nc, chunk, jnp.zeros((1, 1), F32), unroll=4)


def _mlstm(p3, gates_t, conv_w, conv_b, head_g):
    b, s, _ = p3.shape
    nc = s // ML_CHUNK
    cq = _OFF_BQK // ML_QK_DIM
    ck = cq + ML_HEADS
    cv, co, cz = (off // ML_V_DIM for off in (5120, 6144, 7168))
    seq = lambda width, col0: pl.BlockSpec((1, s, width), lambda bi, h: (bi, 0, col0 + h))
    return pl.pallas_call(
        _mlstm_kernel,
        grid=(b, ML_HEADS),
        in_specs=[
            seq(ML_QK_DIM, cq), seq(ML_QK_DIM, ck), seq(ML_V_DIM, cv), seq(ML_V_DIM, co), seq(ML_V_DIM, cz),
            pl.BlockSpec((2 * ML_HEADS, nc, ML_CHUNK), lambda bi, h: (0, bi, 0)),
            pl.BlockSpec((CONV_WIDTH, ML_QK_DIM), lambda bi, h: (0, h)),
            pl.BlockSpec((CONV_WIDTH, ML_QK_DIM), lambda bi, h: (0, ML_HEADS + h)),
            pl.BlockSpec((1, ML_QK_DIM), lambda bi, h: (0, h)),
            pl.BlockSpec((1, ML_QK_DIM), lambda bi, h: (0, ML_HEADS + h)),
            pl.BlockSpec((1, ML_V_DIM), lambda bi, h: (0, 0)),
        ],
        out_specs=pl.BlockSpec((1, s, ML_V_DIM), lambda bi, h: (bi, 0, h)),
        out_shape=jax.ShapeDtypeStruct((b, s, ML_WIDTH), BF16),
        scratch_shapes=[
            pltpu.VMEM((s + 8, ML_QK_DIM), F32),
            pltpu.VMEM((s + 8, ML_QK_DIM), F32),
            pltpu.VMEM((nc, ML_CHUNK), F32),
            pltpu.VMEM((nc, ML_CHUNK), F32),
            pltpu.VMEM((ML_QK_DIM, ML_V_DIM + LANES), F32),
        ],
        compiler_params=pltpu.CompilerParams(
            dimension_semantics=("arbitrary", "arbitrary"), vmem_limit_bytes=VMEM_LIMIT),
        name="mlstm",
    )(p3, p3, p3, p3, p3, gates_t, conv_w, conv_w, conv_b, conv_b, head_g)


def _l0_out_proj_kernel(ya_ref, yb_ref, x_ref, wa_ref, wb_ref, g_ref, h_ref):
    y = _dot(ya_ref[...], wa_ref[...]) + _dot(yb_ref[...], wb_ref[...])
    h_ref[...] = x_ref[...] + y * _rms_scale(y) * g_ref[...]


def _l0_out_proj(ya, yb, x2, wa, wb, post_g, *, tm=512):
    m = x2.shape[0]
    row = lambda width: pl.BlockSpec((tm, width), lambda i: (i, 0))
    const = lambda shape: pl.BlockSpec(shape, lambda i: (0, 0))
    return pl.pallas_call(
        _l0_out_proj_kernel,
        grid=(m // tm,),
        in_specs=[row(DA_WIDTH), row(ML_WIDTH), row(D_MODEL),
                  const((DA_WIDTH, D_MODEL)), const((ML_WIDTH, D_MODEL)), const((1, D_MODEL))],
        out_specs=row(D_MODEL),
        out_shape=jax.ShapeDtypeStruct((m, D_MODEL), F32),
        compiler_params=pltpu.CompilerParams(
            dimension_semantics=("arbitrary",), vmem_limit_bytes=VMEM_LIMIT),
        name="l0_out_proj",
    )(ya, yb, x2, wa, wb, post_g)


def _l1_gmlp_kernel(h_ref, pre_g_ref, wu_ref, wv_ref, wz_ref, sg_g_ref, wsp_ref, bsp_ref,
                    wo_ref, post_g_ref, o_ref, vn_ref, y_ref, *, tm):
    hres = h_ref[...]
    hn = (hres * _rms_scale(hres) * pre_g_ref[...]).astype(BF16)

    v = _gelu_tanh(_dot(hn, wv_ref[...]))
    vn_ref[...] = (v * _rms_scale(v) * sg_g_ref[...]).astype(BF16)

    r_i = lax.broadcasted_iota(jnp.int32, (SG_CHUNK, SG_CHUNK), 0)
    c_i = lax.broadcasted_iota(jnp.int32, (SG_CHUNK, SG_CHUNK), 1)
    causal = c_i <= r_i
    for g in range(SG_GROUPS):
        cols = slice(g * SG_GROUP_DIM, (g + 1) * SG_GROUP_DIM)
        u = _gelu_tanh(_dot(hn, wu_ref[:, cols]))
        gate = _silu(_dot(hn, wz_ref[:, cols]))
        wm = jnp.where(causal, wsp_ref[g], 0.0).astype(BF16)
        bias = bsp_ref[g]
        for c in range(tm // SG_CHUNK):
            rows = slice(c * SG_CHUNK, (c + 1) * SG_CHUNK)
            vs = _dot(wm, vn_ref[rows, cols]) + bias
            y_ref[rows, cols] = (u[rows] * vs * gate[rows]).astype(BF16)

    out = _dot(y_ref[...], wo_ref[...])
    o_ref[...] = hres + out * _rms_scale(out) * post_g_ref[...]


def _l1_gmlp(h1, pre_g, wu, wv, wz, sg_g, w_spatial, b_spatial, wo, post_g, *, tm=256):
    m = h1.shape[0]
    row = pl.BlockSpec((tm, D_MODEL), lambda i: (i, 0))
    const2 = lambda shape: pl.BlockSpec(shape, lambda i: (0, 0), pipeline_mode=pl.Buffered(1))
    const3 = lambda shape: pl.BlockSpec(shape, lambda i: (0, 0, 0), pipeline_mode=pl.Buffered(1))
    return pl.pallas_call(
        functools.partial(_l1_gmlp_kernel, tm=tm),
        grid=(m // tm,),
        in_specs=[
            row, const2((1, D_MODEL)),
            const2((D_MODEL, SG_WIDTH)), const2((D_MODEL, SG_WIDTH)), const2((D_MODEL, SG_WIDTH)),
            const2((1, SG_WIDTH)),
            const3((SG_GROUPS, SG_CHUNK, SG_CHUNK)), const3((SG_GROUPS, SG_CHUNK, 1)),
            const2((SG_WIDTH, D_MODEL)), const2((1, D_MODEL)),
        ],
        out_specs=row,
        out_shape=jax.ShapeDtypeStruct((m, D_MODEL), F32),
        scratch_shapes=[pltpu.VMEM((tm, SG_WIDTH), BF16), pltpu.VMEM((tm, SG_WIDTH), BF16)],
        compiler_params=pltpu.CompilerParams(
            dimension_semantics=("arbitrary",), vmem_limit_bytes=VMEM_LIMIT),
        name="l1_gmlp",
    )(h1, pre_g, wu, wv, wz, sg_g, w_spatial, b_spatial, wo, post_g)


def kernel(x, l0_pre_g, l0_w_in, l0_b_igate, l0_b_fgate, l0_conv_w, l0_conv_b, l0_lambda_q1, l0_lambda_k1, l0_lambda_q2, l0_lambda_k2, l0_da_head_g, l0_ml_head_g, l0_w_out, l0_post_g, l1_pre_g, l1_w_in, l1_sg_norm_g, l1_w_spatial, l1_b_spatial, l1_w_out, l1_post_g):
    b, s, d = x.shape
    m = b * s
    x2 = x.reshape(m, d)
    row = lambda v: v.reshape(1, -1).astype(F32)

    w_main = jnp.concatenate([l0_w_in[:, :_OFF_BI], l0_w_in[:, _OFF_BO:]], axis=1).astype(BF16)
    w_gate = jnp.pad(l0_w_in[:, _OFF_BI:_OFF_BO], ((0, 0), (0, LANES - 2 * ML_HEADS))).astype(BF16)
    b_gate = jnp.pad(jnp.concatenate([l0_b_igate, l0_b_fgate]), (0, LANES - 2 * ML_HEADS)).reshape(1, LANES)
    lam_vecs = jnp.stack([l0_lambda_q1, l0_lambda_k1, l0_lambda_q2, l0_lambda_k2]).astype(F32)

    p, gates_t = _l0_in_proj(x2, row(l0_pre_g), w_main, w_gate, b_gate.astype(F32))
    p3 = p.reshape(b, s, L0_MAIN)
    gates_t = gates_t.reshape(2 * ML_HEADS, m // ML_CHUNK, ML_CHUNK)

    y_a = _diff_attn(p3, lam_vecs, row(l0_da_head_g))
    y_b = _mlstm(p3, gates_t, l0_conv_w.astype(F32), row(l0_conv_b), row(l0_ml_head_g))

    w_out0 = l0_w_out.astype(BF16)
    h1 = _l0_out_proj(y_a.reshape(m, DA_WIDTH), y_b.reshape(m, ML_WIDTH), x2,
                      w_out0[:DA_WIDTH], w_out0[DA_WIDTH:], row(l0_post_g))

    w_in1 = l1_w_in.astype(BF16)
    h2 = _l1_gmlp(h1, row(l1_pre_g),
                  w_in1[:, :SG_WIDTH], w_in1[:, SG_WIDTH:2 * SG_WIDTH], w_in1[:, 2 * SG_WIDTH:],
                  row(l1_sg_norm_g), l1_w_spatial.astype(F32),
                  l1_b_spatial.astype(F32).reshape(SG_GROUPS, SG_CHUNK, 1),
                  l1_w_out.astype(BF16), row(l1_post_g))
    return h2.reshape(b, s, d)
```

```python
import functools
import math

import jax
import jax.numpy as jnp
from jax import lax
from jax.experimental import pallas as pl
from jax.experimental.pallas import tpu as pltpu

F32 = jnp.float32
BF16 = jnp.bfloat16

D_MODEL = 1024
EPS = 1e-6
DA_HEADS = 8
DA_QK_DIM = 64
DA_V_DIM = 128
DA_WIDTH = DA_HEADS * DA_V_DIM
ML_HEADS = 4
ML_QK_DIM = 128
ML_V_DIM = 256
ML_WIDTH = ML_HEADS * ML_V_DIM
ML_CHUNK = 128
CONV_WIDTH = 4
SG_GROUPS = 8
SG_CHUNK = 128
SG_WIDTH = 2 * D_MODEL
SG_GROUP_DIM = SG_WIDTH // SG_GROUPS

LANES = 128
SUM_ROWS = 16
VMEM_LIMIT = 56 * 1024 * 1024
NEG_BIG = -1e30
LOG2E = 1.4426950408889634

_OFF_AQ, _OFF_AK, _OFF_AV, _OFF_AZ = 0, 1024, 2048, 3072
_OFF_BQK, _OFF_BV, _OFF_BI, _OFF_BF, _OFF_BO, _OFF_BZ = 4096, 5120, 6144, 6148, 6152, 7176
L0_MAIN = 8192


def _sigmoid(x):
    return 1.0 / (1.0 + jnp.exp(-x))


def _silu(x):
    return x * _sigmoid(x)


def _gelu_tanh(x):
    c = math.sqrt(2.0 / math.pi)
    return 0.5 * x * (1.0 + jnp.tanh(c * (x + 0.044715 * (x * x * x))))


def _rms_scale(x):
    return lax.rsqrt(jnp.mean(x * x, axis=-1, keepdims=True) + EPS)


def _dot(a, b):
    return jnp.dot(a, b, preferred_element_type=F32)


def _l0_in_proj_kernel(x_ref, g_ref, w_ref, cs_ref, wg_ref, bg_ref, p_ref, gt_ref, hn_ref):
    j = pl.program_id(1)

    @pl.when(j == 0)
    def _():
        x = x_ref[...]
        hn = (x * _rms_scale(x) * g_ref[...]).astype(BF16)
        hn_ref[...] = hn
        gates = _dot(hn, wg_ref[...]) + bg_ref[...]
        gt_ref[...] = gates.T[0:8, :]

    p_ref[...] = (_dot(hn_ref[...], w_ref[...]) * cs_ref[...]).astype(BF16)


def _l0_in_proj(x2, pre_g, w_main, w_gate, b_gate, *, tm=1024, tn=2048):
    m = x2.shape[0]
    q_scale = DA_QK_DIM ** -0.5 * LOG2E
    col_scale = jnp.where(jnp.arange(L0_MAIN) < DA_HEADS * 2 * DA_QK_DIM, q_scale, 1.0).astype(F32).reshape(1, -1)
    return pl.pallas_call(
        _l0_in_proj_kernel,
        grid=(m // tm, L0_MAIN // tn),
        in_specs=[
            pl.BlockSpec((tm, D_MODEL), lambda i, j: (i, 0)),
            pl.BlockSpec((1, D_MODEL), lambda i, j: (0, 0)),
            pl.BlockSpec((D_MODEL, tn), lambda i, j: (0, j)),
            pl.BlockSpec((1, tn), lambda i, j: (0, j)),
            pl.BlockSpec((D_MODEL, LANES), lambda i, j: (0, 0)),
            pl.BlockSpec((1, LANES), lambda i, j: (0, 0)),
        ],
        out_specs=[
            pl.BlockSpec((tm, tn), lambda i, j: (i, j)),
            pl.BlockSpec((8, tm), lambda i, j: (0, i)),
        ],
        out_shape=[
            jax.ShapeDtypeStruct((m, L0_MAIN), BF16),
            jax.ShapeDtypeStruct((8, m), F32),
        ],
        scratch_shapes=[pltpu.VMEM((tm, D_MODEL), BF16)],
        compiler_params=pltpu.CompilerParams(
            dimension_semantics=("arbitrary", "arbitrary"), vmem_limit_bytes=VMEM_LIMIT),
        name="l0_in_proj",
    )(x2, pre_g, w_main, col_scale, w_gate, b_gate)


def _diff_attn_kernel(vq_ref, vk_ref, lam_ref, q_ref, k_ref, v_ref, z_ref, g_ref, o_ref,
                      vt_ref, qbd_ref, acc_ref, m_ref, s0_ref, s1_ref, *, tq, lam_init):
    nq = vt_ref.shape[0]
    tk = tq
    n_off = vq_ref.shape[0]

    lam_v = lam_ref[...]
    lam = (jnp.exp(jnp.sum(lam_v[0:1] * lam_v[1:2], axis=-1, keepdims=True))
           - jnp.exp(jnp.sum(lam_v[2:3] * lam_v[3:4], axis=-1, keepdims=True)) + lam_init)

    ones_row = (lax.broadcasted_iota(jnp.int32, (SUM_ROWS, tk), 0) == 0).astype(F32)
    first = lax.broadcasted_iota(jnp.int32, (LANES, tq), 0) < DA_QK_DIM
    for i in range(nq):
        rows = slice(i * tq, (i + 1) * tq)
        vt = v_ref[0, rows, :].astype(F32).T
        vt_ref[i] = jnp.concatenate([vt, ones_row], axis=0).astype(BF16)
        qt = q_ref[0, rows, :].astype(F32).T
        qbd_ref[i] = jnp.concatenate(
            [jnp.where(first, qt, 0.0), jnp.where(first, 0.0, qt)], axis=1).astype(BF16)

    key = lax.broadcasted_iota(jnp.int32, (tk, 2 * tq), 0)
    qry = lax.broadcasted_iota(jnp.int32, (tk, 2 * tq), 1) & (tq - 1)

    def scores(qi, kb, masked):
        k = k_ref[0, pl.ds(pl.multiple_of(kb * tk, tk), tk), :]
        s = _dot(k, qbd_ref[qi])
        return jnp.where(key <= qry, s, NEG_BIG) if masked else s

    def softmax_pv(s_ref, qi, kb, first):
        s = s_ref[:, 0:2 * tq]
        m_new = jnp.max(s, axis=0, keepdims=True)
        if not first:
            m = m_ref[qi]
            m_new = jnp.maximum(m, m_new)
            alpha = jnp.exp2(m - m_new)
        p = jnp.exp2(s - m_new).astype(BF16)
        pv = _dot(vt_ref[kb], p)
        acc_ref[qi] = pv if first else alpha * acc_ref[qi] + pv
        m_ref[qi] = m_new

    def visit_loop(n, block_of, masked):
        s0_ref[:, 0:2 * tq] = scores(*block_of(0), masked)

        def body(t, carry):
            s1_ref[:, 0:2 * tq] = scores(*block_of(2 * t + 1), masked)
            softmax_pv(s0_ref, *block_of(2 * t), masked)
            s0_ref[:, 0:2 * tq] = scores(*block_of(jnp.minimum(2 * t + 2, n - 1)), masked)
            softmax_pv(s1_ref, *block_of(2 * t + 1), masked)
            return carry

        lax.fori_loop(0, n // 2, body, 0, unroll=2)

    visit_loop(nq, lambda i: (i, i), True)
    visit_loop(n_off, lambda i: (vq_ref[i], vk_ref[i]), False)

    for i in range(nq):
        rows = slice(i * tq, (i + 1) * tq)
        acc = acc_ref[i]
        o_t = acc[0:LANES, :] / acc[LANES:LANES + 1, :]
        o = (o_t[:, 0:tq] - lam * o_t[:, tq:2 * tq]).T
        o = o * _rms_scale(o) * g_ref[...] * (1.0 - lam_init)
        o_ref[0, rows, :] = (o * _silu(z_ref[0, rows, :].astype(F32))).astype(BF16)


def _diff_attn(p3, lam_vecs, head_g, *, tq=512):
    b, s, _ = p3.shape
    nq = s // tq
    hq, hk, hv, hz = (off // LANES for off in (_OFF_AQ, _OFF_AK, _OFF_AV, _OFF_AZ))
    lam_init = 0.8 - 0.6 * math.exp(-0.3 * 0)
    below = [(qi, kb) for qi in range(nq) for kb in range(qi)]
    assert nq % 2 == 0 and len(below) % 2 == 0
    visit_q = jnp.asarray([qi for qi, _ in below], jnp.int32)
    visit_k = jnp.asarray([kb for _, kb in below], jnp.int32)
    seq = lambda col0: pl.BlockSpec((1, s, LANES), lambda bi, h, vq, vk: (bi, 0, col0 + h))
    return pl.pallas_call(
        functools.partial(_diff_attn_kernel, tq=tq, lam_init=lam_init),
        grid_spec=pltpu.PrefetchScalarGridSpec(
            num_scalar_prefetch=2,
            grid=(b, DA_HEADS),
            in_specs=[
                pl.BlockSpec((4, DA_QK_DIM), lambda bi, h, vq, vk: (0, 0)),
                seq(hq), seq(hk), seq(hv), seq(hz),
                pl.BlockSpec((1, DA_V_DIM), lambda bi, h, vq, vk: (0, 0)),
            ],
            out_specs=pl.BlockSpec((1, s, LANES), lambda bi, h, vq, vk: (bi, 0, h)),
            scratch_shapes=[
                pltpu.VMEM((nq, LANES + SUM_ROWS, tq), BF16),
                pltpu.VMEM((nq, LANES, 2 * tq), BF16),
                pltpu.VMEM((nq, LANES + SUM_ROWS, 2 * tq), F32),
                pltpu.VMEM((nq, 1, 2 * tq), F32),
                pltpu.VMEM((tq, 2 * tq), F32),
                pltpu.VMEM((tq, 2 * tq), F32),
            ],
        ),
        out_shape=jax.ShapeDtypeStruct((b, s, DA_WIDTH), BF16),
        compiler_params=pltpu.CompilerParams(
            dimension_semantics=("arbitrary", "arbitrary"), vmem_limit_bytes=VMEM_LIMIT),
        name="diff_attn",
    )(visit_q, visit_k, lam_vecs, p3, p3, p3, p3, head_g)


def _mlstm_kernel(qp_ref, kp_ref, v_ref, o_ref, z_ref, gt_ref, cwq_ref, cwk_ref, cbq_ref, cbk_ref,
                  hg_ref, y_ref, xq_ref, xk_ref, ic_ref, bc_ref, c_ref):
    h = pl.program_id(1)
    s = qp_ref.shape[1]
    L = ML_CHUNK
    nc = s // L
    pad = 8

    zeros = jnp.zeros((pad, ML_QK_DIM), F32)
    xq_ref[0:pad, :] = zeros
    xk_ref[0:pad, :] = zeros
    xq_ref[pad:, :] = qp_ref[0].astype(F32)
    xk_ref[pad:, :] = kp_ref[0].astype(F32)

    ic = gt_ref[h]
    fp = gt_ref[ML_HEADS + h]
    lf = jnp.minimum(fp, 0.0) - jnp.log(1.0 + jnp.exp(-jnp.abs(fp)))
    r_i = lax.broadcasted_iota(jnp.int32, (L, L), 0)
    c_i = lax.broadcasted_iota(jnp.int32, (L, L), 1)
    tri_u = (r_i <= c_i).astype(BF16)
    lf_hi = lf.astype(BF16)
    lf_lo = (lf - lf_hi.astype(F32)).astype(BF16)
    ic_ref[...] = ic
    bc_ref[...] = _dot(lf_hi, tri_u) + _dot(lf_lo, tri_u)

    c_ref[...] = jnp.zeros_like(c_ref)
    causal = c_i <= r_i
    ones_col = (lax.broadcasted_iota(jnp.int32, (L, LANES), 1) == 0).astype(BF16)
    cwq = cwq_ref[...]
    cwk = cwk_ref[...]

    def conv(x_ref, r0, w, bias):
        xw = x_ref[pl.ds(r0, L + pad), :]
        acc = bias
        for j in range(CONV_WIDTH):
            lo = pad - (CONV_WIDTH - 1) + j
            acc = acc + w[j:j + 1, :] * xw[lo:lo + L, :]
        return _silu(acc)

    def chunk(c, m_prev):
        r0 = pl.multiple_of(c * L, L)
        q = (conv(xq_ref, r0, cwq, cbq_ref[...]) * ML_QK_DIM ** -0.5).astype(BF16)
        k_t = conv(xk_ref, r0, cwk, cbk_ref[...]).T
        v_ext = jnp.concatenate([v_ref[0, pl.ds(r0, L), :], ones_col], axis=1)

        ic_row = ic_ref[pl.ds(c, 1), :]
        bc_row = bc_ref[pl.ds(c, 1), :]
        bc_col = jnp.broadcast_to(bc_row, (8, L)).T[:, 0:1]
        b_last = bc_row[:, L - 1:L]

        log_d = jnp.where(causal, bc_col - bc_row + ic_row, NEG_BIG)
        log_inter = bc_col + m_prev
        m_t = jnp.maximum(log_inter, jnp.max(log_d, axis=-1, keepdims=True))
        dmat = jnp.exp(log_d - m_t)
        inter_w = jnp.exp(log_inter - m_t)
        qk = (_dot(q, k_t.astype(BF16)) * dmat).astype(BF16)
        nd = _dot(qk, v_ext) + inter_w * _dot(q, c_ref[...].astype(BF16))
        num = nd[:, 0:ML_V_DIM]
        den = nd[:, ML_V_DIM:ML_V_DIM + 1]
        hm = num / jnp.maximum(jnp.abs(den), jnp.exp(-m_t))
        hm = hm * _sigmoid(o_ref[0, pl.ds(r0, L), :].astype(F32))
        hm = hm * _rms_scale(hm) * hg_ref[...]
        y_ref[0, pl.ds(r0, L), :] = (hm * _silu(z_ref[0, pl.ds(r0, L), :].astype(F32))).astype(BF16)

        g_row = b_last - bc_row + ic_row
        m_new = jnp.maximum(b_last + m_prev, jnp.max(g_row, axis=-1, keepdims=True))
        decay = jnp.exp(b_last + m_prev - m_new)
        w_row = jnp.exp(g_row - m_new)
        c_ref[...] = decay * c_ref[...] + _dot((k_t * w_row).astype(BF16), v_ext)
        return m_new

    lax.fori_loop(0, nc, chunk, jnp.zeros((1, 1), F32), unroll=4)


def _mlstm(p3, gates_t, conv_w, conv_b, head_g):
    b, s, _ = p3.shape
    nc = s // ML_CHUNK
    cq = _OFF_BQK // ML_QK_DIM
    ck = cq + ML_HEADS
    cv, co, cz = (off // ML_V_DIM for off in (5120, 6144, 7168))
    seq = lambda width, col0: pl.BlockSpec((1, s, width), lambda bi, h: (bi, 0, col0 + h))
    return pl.pallas_call(
        _mlstm_kernel,
        grid=(b, ML_HEADS),
        in_specs=[
            seq(ML_QK_DIM, cq), seq(ML_QK_DIM, ck), seq(ML_V_DIM, cv), seq(ML_V_DIM, co), seq(ML_V_DIM, cz),
            pl.BlockSpec((2 * ML_HEADS, nc, ML_CHUNK), lambda bi, h: (0, bi, 0)),
            pl.BlockSpec((CONV_WIDTH, ML_QK_DIM), lambda bi, h: (0, h)),
            pl.BlockSpec((CONV_WIDTH, ML_QK_DIM), lambda bi, h: (0, ML_HEADS + h)),
            pl.BlockSpec((1, ML_QK_DIM), lambda bi, h: (0, h)),
            pl.BlockSpec((1, ML_QK_DIM), lambda bi, h: (0, ML_HEADS + h)),
            pl.BlockSpec((1, ML_V_DIM), lambda bi, h: (0, 0)),
        ],
        out_specs=pl.BlockSpec((1, s, ML_V_DIM), lambda bi, h: (bi, 0, h)),
        out_shape=jax.ShapeDtypeStruct((b, s, ML_WIDTH), BF16),
        scratch_shapes=[
            pltpu.VMEM((s + 8, ML_QK_DIM), F32),
            pltpu.VMEM((s + 8, ML_QK_DIM), F32),
            pltpu.VMEM((nc, ML_CHUNK), F32),
            pltpu.VMEM((nc, ML_CHUNK), F32),
            pltpu.VMEM((ML_QK_DIM, ML_V_DIM + LANES), F32),
        ],
        compiler_params=pltpu.CompilerParams(
            dimension_semantics=("arbitrary", "arbitrary"), vmem_limit_bytes=VMEM_LIMIT),
        name="mlstm",
    )(p3, p3, p3, p3, p3, gates_t, conv_w, conv_w, conv_b, conv_b, head_g)


def _l0_out_proj_kernel(ya_ref, yb_ref, x_ref, wa_ref, wb_ref, g_ref, h_ref):
    y = _dot(ya_ref[...], wa_ref[...]) + _dot(yb_ref[...], wb_ref[...])
    h_ref[...] = x_ref[...] + y * _rms_scale(y) * g_ref[...]


def _l0_out_proj(ya, yb, x2, wa, wb, post_g, *, tm=512):
    m = x2.shape[0]
    row = lambda width: pl.BlockSpec((tm, width), lambda i: (i, 0))
    const = lambda shape: pl.BlockSpec(shape, lambda i: (0, 0))
    return pl.pallas_call(
        _l0_out_proj_kernel,
        grid=(m // tm,),
        in_specs=[row(DA_WIDTH), row(ML_WIDTH), row(D_MODEL),
                  const((DA_WIDTH, D_MODEL)), const((ML_WIDTH, D_MODEL)), const((1, D_MODEL))],
        out_specs=row(D_MODEL),
        out_shape=jax.ShapeDtypeStruct((m, D_MODEL), F32),
        compiler_params=pltpu.CompilerParams(
            dimension_semantics=("arbitrary",), vmem_limit_bytes=VMEM_LIMIT),
        name="l0_out_proj",
    )(ya, yb, x2, wa, wb, post_g)


def _l1_gmlp_kernel(h_ref, pre_g_ref, wu_ref, wv_ref, wz_ref, sg_g_ref, wsp_ref, bsp_ref,
                    wo_ref, post_g_ref, o_ref, vn_ref, y_ref, *, tm):
    hres = h_ref[...]
    hn = (hres * _rms_scale(hres) * pre_g_ref[...]).astype(BF16)

    v = _gelu_tanh(_dot(hn, wv_ref[...]))
    vn_ref[...] = (v * _rms_scale(v) * sg_g_ref[...]).astype(BF16)

    r_i = lax.broadcasted_iota(jnp.int32, (SG_CHUNK, SG_CHUNK), 0)
    c_i = lax.broadcasted_iota(jnp.int32, (SG_CHUNK, SG_CHUNK), 1)
    causal = c_i <= r_i
    for g in range(SG_GROUPS):
        cols = slice(g * SG_GROUP_DIM, (g + 1) * SG_GROUP_DIM)
        u = _gelu_tanh(_dot(hn, wu_ref[:, cols]))
        gate = _silu(_dot(hn, wz_ref[:, cols]))
        wm = jnp.where(causal, wsp_ref[g], 0.0).astype(BF16)
        bias = bsp_ref[g]
        for c in range(tm // SG_CHUNK):
            rows = slice(c * SG_CHUNK, (c + 1) * SG_CHUNK)
            vs = _dot(wm, vn_ref[rows, cols]) + bias
            y_ref[rows, cols] = (u[rows] * vs * gate[rows]).astype(BF16)

    out = _dot(y_ref[...], wo_ref[...])
    o_ref[...] = hres + out * _rms_scale(out) * post_g_ref[...]


def _l1_gmlp(h1, pre_g, wu, wv, wz, sg_g, w_spatial, b_spatial, wo, post_g, *, tm=512):
    m = h1.shape[0]
    row = pl.BlockSpec((tm, D_MODEL), lambda i: (i, 0))
    const2 = lambda shape: pl.BlockSpec(shape, lambda i: (0, 0), pipeline_mode=pl.Buffered(1))
    const3 = lambda shape: pl.BlockSpec(shape, lambda i: (0, 0, 0), pipeline_mode=pl.Buffered(1))
    return pl.pallas_call(
        functools.partial(_l1_gmlp_kernel, tm=tm),
        grid=(m // tm,),
        in_specs=[
            row, const2((1, D_MODEL)),
            const2((D_MODEL, SG_WIDTH)), const2((D_MODEL, SG_WIDTH)), const2((D_MODEL, SG_WIDTH)),
            const2((1, SG_WIDTH)),
            const3((SG_GROUPS, SG_CHUNK, SG_CHUNK)), const3((SG_GROUPS, SG_CHUNK, 1)),
            const2((SG_WIDTH, D_MODEL)), const2((1, D_MODEL)),
        ],
        out_specs=row,
        out_shape=jax.ShapeDtypeStruct((m, D_MODEL), F32),
        scratch_shapes=[pltpu.VMEM((tm, SG_WIDTH), BF16), pltpu.VMEM((tm, SG_WIDTH), BF16)],
        compiler_params=pltpu.CompilerParams(
            dimension_semantics=("arbitrary",), vmem_limit_bytes=VMEM_LIMIT),
        name="l1_gmlp",
    )(h1, pre_g, wu, wv, wz, sg_g, w_spatial, b_spatial, wo, post_g)


def kernel(x, l0_pre_g, l0_w_in, l0_b_igate, l0_b_fgate, l0_conv_w, l0_conv_b, l0_lambda_q1, l0_lambda_k1, l0_lambda_q2, l0_lambda_k2, l0_da_head_g, l0_ml_head_g, l0_w_out, l0_post_g, l1_pre_g, l1_w_in, l1_sg_norm_g, l1_w_spatial, l1_b_spatial, l1_w_out, l1_post_g):
    b, s, d = x.shape
    m = b * s
    x2 = x.reshape(m, d)
    row = lambda v: v.reshape(1, -1).astype(F32)

    w_main = jnp.concatenate([l0_w_in[:, :_OFF_BI], l0_w_in[:, _OFF_BO:]], axis=1).astype(BF16)
    w_gate = jnp.pad(l0_w_in[:, _OFF_BI:_OFF_BO], ((0, 0), (0, LANES - 2 * ML_HEADS))).astype(BF16)
    b_gate = jnp.pad(jnp.concatenate([l0_b_igate, l0_b_fgate]), (0, LANES - 2 * ML_HEADS)).reshape(1, LANES)
    lam_vecs = jnp.stack([l0_lambda_q1, l0_lambda_k1, l0_lambda_q2, l0_lambda_k2]).astype(F32)

    p, gates_t = _l0_in_proj(x2, row(l0_pre_g), w_main, w_gate, b_gate.astype(F32))
    p3 = p.reshape(b, s, L0_MAIN)
    gates_t = gates_t.reshape(2 * ML_HEADS, m // ML_CHUNK, ML_CHUNK)

    y_a = _diff_attn(p3, lam_vecs, row(l0_da_head_g))
    y_b = _mlstm(p3, gates_t, l0_conv_w.astype(F32), row(l0_conv_b), row(l0_ml_head_g))

    w_out0 = l0_w_out.astype(BF16)
    h1 = _l0_out_proj(y_a.reshape(m, DA_WIDTH), y_b.reshape(m, ML_WIDTH), x2,
                      w_out0[:DA_WIDTH], w_out0[DA_WIDTH:], row(l0_post_g))

    w_in1 = l1_w_in.astype(BF16)
    h2 = _l1_gmlp(h1, row(l1_pre_g),
                  w_in1[:, :SG_WIDTH], w_in1[:, SG_WIDTH:2 * SG_WIDTH], w_in1[:, 2 * SG_WIDTH:],
                  row(l1_sg_norm_g), l1_w_spatial.astype(F32),
                  l1_b_spatial.astype(F32).reshape(SG_GROUPS, SG_CHUNK, 1),
                  l1_w_out.astype(BF16), row(l1_post_g))
    return h2.reshape(b, s, d)
```

```python
import functools
import math

import jax
import jax.numpy as jnp
from jax import lax
from jax.experimental import pallas as pl
from jax.experimental.pallas import tpu as pltpu

F32 = jnp.float32
BF16 = jnp.bfloat16

D_MODEL = 1024
EPS = 1e-6
DA_HEADS = 8
DA_QK_DIM = 64
DA_V_DIM = 128
DA_WIDTH = DA_HEADS * DA_V_DIM
ML_HEADS = 4
ML_QK_DIM = 128
ML_V_DIM = 256
ML_WIDTH = ML_HEADS * ML_V_DIM
ML_CHUNK = 128
CONV_WIDTH = 4
SG_GROUPS = 8
SG_CHUNK = 128
SG_WIDTH = 2 * D_MODEL
SG_GROUP_DIM = SG_WIDTH // SG_GROUPS

LANES = 128
SUM_ROWS = 16
VMEM_LIMIT = 56 * 1024 * 1024
NEG_BIG = -1e30
LOG2E = 1.4426950408889634

_OFF_AQ, _OFF_AK, _OFF_AV, _OFF_AZ = 0, 1024, 2048, 3072
_OFF_BQK, _OFF_BV, _OFF_BI, _OFF_BF, _OFF_BO, _OFF_BZ = 4096, 5120, 6144, 6148, 6152, 7176
L0_MAIN = 8192


def _sigmoid(x):
    return 1.0 / (1.0 + jnp.exp(-x))


def _silu(x):
    return x * _sigmoid(x)


def _gelu_tanh(x):
    c = math.sqrt(2.0 / math.pi)
    return 0.5 * x * (1.0 + jnp.tanh(c * (x + 0.044715 * (x * x * x))))


def _rms_scale(x):
    return lax.rsqrt(jnp.mean(x * x, axis=-1, keepdims=True) + EPS)


def _dot(a, b):
    return jnp.dot(a, b, preferred_element_type=F32)


def _l0_in_proj_kernel(x_ref, g_ref, w_ref, cs_ref, wg_ref, bg_ref, p_ref, gt_ref, hn_ref):
    j = pl.program_id(1)

    @pl.when(j == 0)
    def _():
        x = x_ref[...]
        hn = (x * _rms_scale(x) * g_ref[...]).astype(BF16)
        hn_ref[...] = hn
        gates = _dot(hn, wg_ref[...]) + bg_ref[...]
        gt_ref[...] = gates.T[0:8, :]

    p_ref[...] = (_dot(hn_ref[...], w_ref[...]) * cs_ref[...]).astype(BF16)


def _l0_in_proj(x2, pre_g, w_main, w_gate, b_gate, *, tm=1024, tn=2048):
    m = x2.shape[0]
    q_scale = DA_QK_DIM ** -0.5 * LOG2E
    col_scale = jnp.where(jnp.arange(L0_MAIN) < DA_HEADS * 2 * DA_QK_DIM, q_scale, 1.0).astype(F32).reshape(1, -1)
    return pl.pallas_call(
        _l0_in_proj_kernel,
        grid=(m // tm, L0_MAIN // tn),
        in_specs=[
            pl.BlockSpec((tm, D_MODEL), lambda i, j: (i, 0)),
            pl.BlockSpec((1, D_MODEL), lambda i, j: (0, 0)),
            pl.BlockSpec((D_MODEL, tn), lambda i, j: (0, j)),
            pl.BlockSpec((1, tn), lambda i, j: (0, j)),
            pl.BlockSpec((D_MODEL, LANES), lambda i, j: (0, 0)),
            pl.BlockSpec((1, LANES), lambda i, j: (0, 0)),
        ],
        out_specs=[
            pl.BlockSpec((tm, tn), lambda i, j: (i, j)),
            pl.BlockSpec((8, tm), lambda i, j: (0, i)),
        ],
        out_shape=[
            jax.ShapeDtypeStruct((m, L0_MAIN), BF16),
            jax.ShapeDtypeStruct((8, m), F32),
        ],
        scratch_shapes=[pltpu.VMEM((tm, D_MODEL), BF16)],
        compiler_params=pltpu.CompilerParams(
            dimension_semantics=("arbitrary", "arbitrary"), vmem_limit_bytes=VMEM_LIMIT),
        name="l0_in_proj",
    )(x2, pre_g, w_main, col_scale, w_gate, b_gate)


def _diff_attn_kernel(vq_ref, vk_ref, lam_ref, q_ref, k_ref, v_ref, z_ref, g_ref, o_ref,
                      vt_ref, qbd_ref, acc_ref, m_ref, s0_ref, s1_ref, cmax0_ref, cmax1_ref, *, tq, lam_init):
    nq = vt_ref.shape[0]
    tk = tq
    n_off = vq_ref.shape[0]

    lam_v = lam_ref[...]
    lam = (jnp.exp(jnp.sum(lam_v[0:1] * lam_v[1:2], axis=-1, keepdims=True))
           - jnp.exp(jnp.sum(lam_v[2:3] * lam_v[3:4], axis=-1, keepdims=True)) + lam_init)

    half = tq // 2
    ones_row = (lax.broadcasted_iota(jnp.int32, (SUM_ROWS, tk), 0) == 0).astype(F32)
    first = lax.broadcasted_iota(jnp.int32, (LANES, half), 0) < DA_QK_DIM
    for i in range(nq):
        rows = slice(i * tq, (i + 1) * tq)
        vt = v_ref[0, rows, :].astype(F32).T
        vt_ref[i] = jnp.concatenate([vt, ones_row], axis=0).astype(BF16)
        qt = q_ref[0, rows, :].astype(F32).T
        parts = []
        for qh in (qt[:, 0:half], qt[:, half:tq]):
            parts += [jnp.where(first, qh, 0.0), jnp.where(first, 0.0, qh)]
        qbd_ref[i] = jnp.concatenate(parts, axis=1).astype(BF16)

    buf0, buf1 = (s0_ref, cmax0_ref), (s1_ref, cmax1_ref)

    def pipelined(n, issue, consume):
        issue(buf0, 0)

        def body(t, carry):
            issue(buf1, 2 * t + 1)
            consume(buf0, 2 * t)
            issue(buf0, jnp.minimum(2 * t + 2, n - 1))
            consume(buf1, 2 * t + 1)
            return carry

        lax.fori_loop(0, n // 2, body, 0, unroll=2)

    tri = (lax.broadcasted_iota(jnp.int32, (half, tq), 0)
           <= (lax.broadcasted_iota(jnp.int32, (half, tq), 1) & (half - 1)))

    def diag_scores(buf, i):
        s_ref, cmax_ref = buf
        r0 = pl.multiple_of(i * tk, tk)
        s_a = _dot(k_ref[0, pl.ds(r0, half), :], qbd_ref[i])
        s_b = _dot(k_ref[0, pl.ds(r0 + half, half), :], qbd_ref[i, :, tq:2 * tq])
        left = jnp.where(tri, s_a[:, 0:tq], NEG_BIG)
        right = s_a[:, tq:2 * tq]
        s_b = jnp.where(tri, s_b, NEG_BIG)
        s_ref[0:half, 0:tq] = left
        s_ref[0:half, tq:2 * tq] = right
        s_ref[half:tq, tq:2 * tq] = s_b
        cmax_ref[:, 0:tq] = jnp.max(left, axis=0, keepdims=True)
        cmax_ref[:, tq:2 * tq] = jnp.maximum(jnp.max(right, axis=0, keepdims=True),
                                             jnp.max(s_b, axis=0, keepdims=True))

    def diag_softmax_pv(buf, i):
        s_ref, cmax_ref = buf
        m_new = cmax_ref[...]
        p_a = jnp.exp2(s_ref[0:half, :] - m_new).astype(BF16)
        p_b = jnp.exp2(s_ref[half:tq, tq:2 * tq] - m_new[:, tq:2 * tq]).astype(BF16)
        pv_a = _dot(vt_ref[i, :, 0:half], p_a)
        pv_b = _dot(vt_ref[i, :, half:tq], p_b)
        acc_ref[i, :, 0:tq] = pv_a[:, 0:tq]
        acc_ref[i, :, tq:2 * tq] = pv_a[:, tq:2 * tq] + pv_b
        m_ref[i] = m_new

    def scores(buf, i):
        s_ref, cmax_ref = buf
        k = k_ref[0, pl.ds(pl.multiple_of(vk_ref[i] * tk, tk), tk), :]
        s = _dot(k, qbd_ref[vq_ref[i]])
        s_ref[...] = s
        cmax_ref[...] = jnp.max(s, axis=0, keepdims=True)

    def softmax_pv(buf, i):
        s_ref, cmax_ref = buf
        qi = vq_ref[i]
        m = m_ref[qi]
        m_new = jnp.maximum(m, cmax_ref[...])
        alpha = jnp.exp2(m - m_new)
        p = jnp.exp2(s_ref[...] - m_new).astype(BF16)
        acc_ref[qi] = alpha * acc_ref[qi] + _dot(vt_ref[vk_ref[i]], p)
        m_ref[qi] = m_new

    pipelined(nq, diag_scores, diag_softmax_pv)
    pipelined(n_off, scores, softmax_pv)

    for i in range(nq):
        acc = acc_ref[i]
        o_t = acc[0:LANES, :] / acc[LANES:LANES + 1, :]
        o1 = jnp.concatenate([o_t[:, 0:half], o_t[:, tq:tq + half]], axis=1)
        o2 = jnp.concatenate([o_t[:, half:tq], o_t[:, tq + half:2 * tq]], axis=1)
        o = (o1 - lam * o2).T
        o = o * _rms_scale(o) * g_ref[...] * (1.0 - lam_init)
        rows = slice(i * tq, (i + 1) * tq)
        o_ref[0, rows, :] = (o * _silu(z_ref[0, rows, :].astype(F32))).astype(BF16)


def _diff_attn(p3, lam_vecs, head_g, *, tq=512):
    b, s, _ = p3.shape
    nq = s // tq
    hq, hk, hv, hz = (off // LANES for off in (_OFF_AQ, _OFF_AK, _OFF_AV, _OFF_AZ))
    lam_init = 0.8 - 0.6 * math.exp(-0.3 * 0)
    below = [(qi, kb) for qi in range(nq) for kb in range(qi)]
    assert nq % 2 == 0 and len(below) % 2 == 0
    visit_q = jnp.asarray([qi for qi, _ in below], jnp.int32)
    visit_k = jnp.asarray([kb for _, kb in below], jnp.int32)
    seq = lambda col0: pl.BlockSpec((1, s, LANES), lambda bi, h, vq, vk: (bi, 0, col0 + h))
    return pl.pallas_call(
        functools.partial(_diff_attn_kernel, tq=tq, lam_init=lam_init),
        grid_spec=pltpu.PrefetchScalarGridSpec(
            num_scalar_prefetch=2,
            grid=(b, DA_HEADS),
            in_specs=[
                pl.BlockSpec((4, DA_QK_DIM), lambda bi, h, vq, vk: (0, 0)),
                seq(hq), seq(hk), seq(hv), seq(hz),
                pl.BlockSpec((1, DA_V_DIM), lambda bi, h, vq, vk: (0, 0)),
            ],
            out_specs=pl.BlockSpec((1, s, LANES), lambda bi, h, vq, vk: (bi, 0, h)),
            scratch_shapes=[
                pltpu.VMEM((nq, LANES + SUM_ROWS, tq), BF16),
                pltpu.VMEM((nq, LANES, 2 * tq), BF16),
                pltpu.VMEM((nq, LANES + SUM_ROWS, 2 * tq), F32),
                pltpu.VMEM((nq, 1, 2 * tq), F32),
                pltpu.VMEM((tq, 2 * tq), F32),
                pltpu.VMEM((tq, 2 * tq), F32),
                pltpu.VMEM((1, 2 * tq), F32),
                pltpu.VMEM((1, 2 * tq), F32),
            ],
        ),
        out_shape=jax.ShapeDtypeStruct((b, s, DA_WIDTH), BF16),
        compiler_params=pltpu.CompilerParams(
            dimension_semantics=("arbitrary", "arbitrary"), vmem_limit_bytes=VMEM_LIMIT),
        name="diff_attn",
    )(visit_q, visit_k, lam_vecs, p3, p3, p3, p3, head_g)


def _mlstm_kernel(qp_ref, kp_ref, v_ref, o_ref, z_ref, gt_ref, cwq_ref, cwk_ref, cbq_ref, cbk_ref,
                  hg_ref, y_ref, xq_ref, xk_ref, ic_ref, bc_ref, c_ref):
    h = pl.program_id(1)
    s = qp_ref.shape[1]
    L = ML_CHUNK
    nc = s // L
    pad = 8

    zeros = jnp.zeros((pad, ML_QK_DIM), F32)
    xq_ref[0:pad, :] = zeros
    xk_ref[0:pad, :] = zeros
    xq_ref[pad:, :] = qp_ref[0].astype(F32)
    xk_ref[pad:, :] = kp_ref[0].astype(F32)

    ic = gt_ref[h]
    fp = gt_ref[ML_HEADS + h]
    lf = jnp.minimum(fp, 0.0) - jnp.log(1.0 + jnp.exp(-jnp.abs(fp)))
    r_i = lax.broadcasted_iota(jnp.int32, (L, L), 0)
    c_i = lax.broadcasted_iota(jnp.int32, (L, L), 1)
    tri_u = (r_i <= c_i).astype(BF16)
    lf_hi = lf.astype(BF16)
    lf_lo = (lf - lf_hi.astype(F32)).astype(BF16)
    ic_ref[...] = ic
    bc_ref[...] = _dot(lf_hi, tri_u) + _dot(lf_lo, tri_u)

    c_ref[...] = jnp.zeros_like(c_ref)
    causal = c_i <= r_i
    ones_col = (lax.broadcasted_iota(jnp.int32, (L, LANES), 1) == 0).astype(BF16)
    cwq = cwq_ref[...]
    cwk = cwk_ref[...]

    def conv(x_ref, r0, w, bias):
        xw = x_ref[pl.ds(r0, L + pad), :]
        acc = bias
        for j in range(CONV_WIDTH):
            lo = pad - (CONV_WIDTH - 1) + j
            acc = acc + w[j:j + 1, :] * xw[lo:lo + L, :]
        return _silu(acc)

    def chunk(c, m_prev):
        r0 = pl.multiple_of(c * L, L)
        q = (conv(xq_ref, r0, cwq, cbq_ref[...]) * ML_QK_DIM ** -0.5).astype(BF16)
        k_t = conv(xk_ref, r0, cwk, cbk_ref[...]).T
        v_ext = jnp.concatenate([v_ref[0, pl.ds(r0, L), :], ones_col], axis=1)

        ic_row = ic_ref[pl.ds(c, 1), :]
        bc_row = bc_ref[pl.ds(c, 1), :]
        bc_col = jnp.broadcast_to(bc_row, (8, L)).T[:, 0:1]
        b_last = bc_row[:, L - 1:L]

        log_d = jnp.where(causal, bc_col - bc_row + ic_row, NEG_BIG)
        log_inter = bc_col + m_prev
        m_t = jnp.maximum(log_inter, jnp.max(log_d, axis=-1, keepdims=True))
        dmat = jnp.exp(log_d - m_t)
        inter_w = jnp.exp(log_inter - m_t)
        qk = (_dot(q, k_t.astype(BF16)) * dmat).astype(BF16)
        nd = _dot(qk, v_ext) + inter_w * _dot(q, c_ref[...].astype(BF16))
        num = nd[:, 0:ML_V_DIM]
        den = nd[:, ML_V_DIM:ML_V_DIM + 1]
        hm = num / jnp.maximum(jnp.abs(den), jnp.exp(-m_t))
        hm = hm * _sigmoid(o_ref[0, pl.ds(r0, L), :].astype(F32))
        hm = hm * _rms_scale(hm) * hg_ref[...]
        y_ref[0, pl.ds(r0, L), :] = (hm * _silu(z_ref[0, pl.ds(r0, L), :].astype(F32))).astype(BF16)

        g_row = b_last - bc_row + ic_row
        m_new = jnp.maximum(b_last + m_prev, jnp.max(g_row, axis=-1, keepdims=True))
        decay = jnp.exp(b_last + m_prev - m_new)
        w_row = jnp.exp(g_row - m_new)
        c_ref[...] = decay * c_ref[...] + _dot((k_t * w_row).astype(BF16), v_ext)
        return m_new

    lax.fori_loop(0, nc, chunk, jnp.zeros((1, 1), F32), unroll=8)


def _mlstm(p3, gates_t, conv_w, conv_b, head_g):
    b, s, _ = p3.shape
    nc = s // ML_CHUNK
    cq = _OFF_BQK // ML_QK_DIM
    ck = cq + ML_HEADS
    cv, co, cz = (off // ML_V_DIM for off in (5120, 6144, 7168))
    seq = lambda width, col0: pl.BlockSpec((1, s, width), lambda bi, h: (bi, 0, col0 + h))
    return pl.pallas_call(
        _mlstm_kernel,
        grid=(b, ML_HEADS),
        in_specs=[
            seq(ML_QK_DIM, cq), seq(ML_QK_DIM, ck), seq(ML_V_DIM, cv), seq(ML_V_DIM, co), seq(ML_V_DIM, cz),
            pl.BlockSpec((2 * ML_HEADS, nc, ML_CHUNK), lambda bi, h: (0, bi, 0)),
            pl.BlockSpec((CONV_WIDTH, ML_QK_DIM), lambda bi, h: (0, h)),
            pl.BlockSpec((CONV_WIDTH, ML_QK_DIM), lambda bi, h: (0, ML_HEADS + h)),
            pl.BlockSpec((1, ML_QK_DIM), lambda bi, h: (0, h)),
            pl.BlockSpec((1, ML_QK_DIM), lambda bi, h: (0, ML_HEADS + h)),
            pl.BlockSpec((1, ML_V_DIM), lambda bi, h: (0, 0)),
        ],
        out_specs=pl.BlockSpec((1, s, ML_V_DIM), lambda bi, h: (bi, 0, h)),
        out_shape=jax.ShapeDtypeStruct((b, s, ML_WIDTH), BF16),
        scratch_shapes=[
            pltpu.VMEM((s + 8, ML_QK_DIM), F32),
            pltpu.VMEM((s + 8, ML_QK_DIM), F32),
            pltpu.VMEM((nc, ML_CHUNK), F32),
            pltpu.VMEM((nc, ML_CHUNK), F32),
            pltpu.VMEM((ML_QK_DIM, ML_V_DIM + LANES), F32),
        ],
        compiler_params=pltpu.CompilerParams(
            dimension_semantics=("arbitrary", "arbitrary"), vmem_limit_bytes=VMEM_LIMIT),
        name="mlstm",
    )(p3, p3, p3, p3, p3, gates_t, conv_w, conv_w, conv_b, conv_b, head_g)


def _l0_out_proj_kernel(ya_ref, yb_ref, x_ref, wa_ref, wb_ref, g_ref, h_ref):
    y = _dot(ya_ref[...], wa_ref[...]) + _dot(yb_ref[...], wb_ref[...])
    h_ref[...] = x_ref[...] + y * _rms_scale(y) * g_ref[...]


def _l0_out_proj(ya, yb, x2, wa, wb, post_g, *, tm=512):
    m = x2.shape[0]
    row = lambda width: pl.BlockSpec((tm, width), lambda i: (i, 0))
    const = lambda shape: pl.BlockSpec(shape, lambda i: (0, 0))
    return pl.pallas_call(
        _l0_out_proj_kernel,
        grid=(m // tm,),
        in_specs=[row(DA_WIDTH), row(ML_WIDTH), row(D_MODEL),
                  const((DA_WIDTH, D_MODEL)), const((ML_WIDTH, D_MODEL)), const((1, D_MODEL))],
        out_specs=row(D_MODEL),
        out_shape=jax.ShapeDtypeStruct((m, D_MODEL), F32),
        compiler_params=pltpu.CompilerParams(
            dimension_semantics=("arbitrary",), vmem_limit_bytes=VMEM_LIMIT),
        name="l0_out_proj",
    )(ya, yb, x2, wa, wb, post_g)


def _l1_gmlp_kernel(h_ref, pre_g_ref, wu_ref, wv_ref, wz_ref, sg_g_ref, wsp_ref, bsp_ref,
                    wo_ref, post_g_ref, o_ref, vn_ref, y_ref, *, tm):
    hres = h_ref[...]
    hn = (hres * _rms_scale(hres) * pre_g_ref[...]).astype(BF16)

    v = _gelu_tanh(_dot(hn, wv_ref[...]))
    vn_ref[...] = (v * _rms_scale(v) * sg_g_ref[...]).astype(BF16)

    r_i = lax.broadcasted_iota(jnp.int32, (SG_CHUNK, SG_CHUNK), 0)
    c_i = lax.broadcasted_iota(jnp.int32, (SG_CHUNK, SG_CHUNK), 1)
    causal = c_i <= r_i
    for g in range(SG_GROUPS):
        cols = slice(g * SG_GROUP_DIM, (g + 1) * SG_GROUP_DIM)
        u = _gelu_tanh(_dot(hn, wu_ref[:, cols]))
        gate = _silu(_dot(hn, wz_ref[:, cols]))
        wm = jnp.where(causal, wsp_ref[g], 0.0).astype(BF16)
        bias = bsp_ref[g]
        for c in range(tm // SG_CHUNK):
            rows = slice(c * SG_CHUNK, (c + 1) * SG_CHUNK)
            vs = _dot(wm, vn_ref[rows, cols]) + bias
            y_ref[rows, cols] = (u[rows] * vs * gate[rows]).astype(BF16)

    out = _dot(y_ref[...], wo_ref[...])
    o_ref[...] = hres + out * _rms_scale(out) * post_g_ref[...]


def _l1_gmlp(h1, pre_g, wu, wv, wz, sg_g, w_spatial, b_spatial, wo, post_g, *, tm=512):
    m = h1.shape[0]
    row = pl.BlockSpec((tm, D_MODEL), lambda i: (i, 0))
    const2 = lambda shape: pl.BlockSpec(shape, lambda i: (0, 0), pipeline_mode=pl.Buffered(1))
    const3 = lambda shape: pl.BlockSpec(shape, lambda i: (0, 0, 0), pipeline_mode=pl.Buffered(1))
    return pl.pallas_call(
        functools.partial(_l1_gmlp_kernel, tm=tm),
        grid=(m // tm,),
        in_specs=[
            row, const2((1, D_MODEL)),
            const2((D_MODEL, SG_WIDTH)), const2((D_MODEL, SG_WIDTH)), const2((D_MODEL, SG_WIDTH)),
            const2((1, SG_WIDTH)),
            const3((SG_GROUPS, SG_CHUNK, SG_CHUNK)), const3((SG_GROUPS, SG_CHUNK, 1)),
            const2((SG_WIDTH, D_MODEL)), const2((1, D_MODEL)),
        ],
        out_specs=row,
        out_shape=jax.ShapeDtypeStruct((m, D_MODEL), F32),
        scratch_shapes=[pltpu.VMEM((tm, SG_WIDTH), BF16), pltpu.VMEM((tm, SG_WIDTH), BF16)],
        compiler_params=pltpu.CompilerParams(
            dimension_semantics=("arbitrary",), vmem_limit_bytes=VMEM_LIMIT),
        name="l1_gmlp",
    )(h1, pre_g, wu, wv, wz, sg_g, w_spatial, b_spatial, wo, post_g)


def kernel(x, l0_pre_g, l0_w_in, l0_b_igate, l0_b_fgate, l0_conv_w, l0_conv_b, l0_lambda_q1, l0_lambda_k1, l0_lambda_q2, l0_lambda_k2, l0_da_head_g, l0_ml_head_g, l0_w_out, l0_post_g, l1_pre_g, l1_w_in, l1_sg_norm_g, l1_w_spatial, l1_b_spatial, l1_w_out, l1_post_g):
    b, s, d = x.shape
    m = b * s
    x2 = x.reshape(m, d)
    row = lambda v: v.reshape(1, -1).astype(F32)

    w_main = jnp.concatenate([l0_w_in[:, :_OFF_BI], l0_w_in[:, _OFF_BO:]], axis=1).astype(BF16)
    w_gate = jnp.pad(l0_w_in[:, _OFF_BI:_OFF_BO], ((0, 0), (0, LANES - 2 * ML_HEADS))).astype(BF16)
    b_gate = jnp.pad(jnp.concatenate([l0_b_igate, l0_b_fgate]), (0, LANES - 2 * ML_HEADS)).reshape(1, LANES)
    lam_vecs = jnp.stack([l0_lambda_q1, l0_lambda_k1, l0_lambda_q2, l0_lambda_k2]).astype(F32)

    p, gates_t = _l0_in_proj(x2, row(l0_pre_g), w_main, w_gate, b_gate.astype(F32))
    p3 = p.reshape(b, s, L0_MAIN)
    gates_t = gates_t.reshape(2 * ML_HEADS, m // ML_CHUNK, ML_CHUNK)

    y_a = _diff_attn(p3, lam_vecs, row(l0_da_head_g))
    y_b = _mlstm(p3, gates_t, l0_conv_w.astype(F32), row(l0_conv_b), row(l0_ml_head_g))

    w_out0 = l0_w_out.astype(BF16)
    h1 = _l0_out_proj(y_a.reshape(m, DA_WIDTH), y_b.reshape(m, ML_WIDTH), x2,
                      w_out0[:DA_WIDTH], w_out0[DA_WIDTH:], row(l0_post_g))

    w_in1 = l1_w_in.astype(BF16)
    h2 = _l1_gmlp(h1, row(l1_pre_g),
                  w_in1[:, :SG_WIDTH], w_in1[:, SG_WIDTH:2 * SG_WIDTH], w_in1[:, 2 * SG_WIDTH:],
                  row(l1_sg_norm_g), l1_w_spatial.astype(F32),
                  l1_b_spatial.astype(F32).reshape(SG_GROUPS, SG_CHUNK, 1),
                  l1_w_out.astype(BF16), row(l1_post_g))
    return h2.reshape(b, s, d)
```

```python
import functools
import math

import jax
import jax.numpy as jnp
from jax import lax
from jax.experimental import pallas as pl
from jax.experimental.pallas import tpu as pltpu

F32 = jnp.float32
BF16 = jnp.bfloat16

D_MODEL = 1024
EPS = 1e-6
DA_HEADS = 8
DA_QK_DIM = 64
DA_V_DIM = 128
DA_WIDTH = DA_HEADS * DA_V_DIM
ML_HEADS = 4
ML_QK_DIM = 128
ML_V_DIM = 256
ML_WIDTH = ML_HEADS * ML_V_DIM
ML_CHUNK = 128
CONV_WIDTH = 4
SG_GROUPS = 8
SG_CHUNK = 128
SG_WIDTH = 2 * D_MODEL
SG_GROUP_DIM = SG_WIDTH // SG_GROUPS

LANES = 128
SUM_ROWS = 16
VMEM_LIMIT = 56 * 1024 * 1024
NEG_BIG = -1e30
LOG2E = 1.4426950408889634

_OFF_AQ, _OFF_AK, _OFF_AV, _OFF_AZ = 0, 1024, 2048, 3072
_OFF_BQK, _OFF_BV, _OFF_BI, _OFF_BF, _OFF_BO, _OFF_BZ = 4096, 5120, 6144, 6148, 6152, 7176
L0_MAIN = 8192


def _sigmoid(x):
    return 1.0 / (1.0 + jnp.exp(-x))


def _silu(x):
    return x * _sigmoid(x)


def _gelu_tanh(x):
    c = math.sqrt(2.0 / math.pi)
    return 0.5 * x * (1.0 + jnp.tanh(c * (x + 0.044715 * (x * x * x))))


def _rms_scale(x):
    return lax.rsqrt(jnp.mean(x * x, axis=-1, keepdims=True) + EPS)


def _dot(a, b):
    return jnp.dot(a, b, preferred_element_type=F32)


def _l0_in_proj_kernel(x_ref, g_ref, w_ref, cs_ref, wg_ref, bg_ref, p_ref, gt_ref, hn_ref):
    j = pl.program_id(1)

    @pl.when(j == 0)
    def _():
        x = x_ref[...]
        hn = (x * _rms_scale(x) * g_ref[...]).astype(BF16)
        hn_ref[...] = hn
        gates = _dot(hn, wg_ref[...]) + bg_ref[...]
        gt_ref[...] = gates.T[0:8, :]

    p_ref[...] = (_dot(hn_ref[...], w_ref[...]) * cs_ref[...]).astype(BF16)


def _l0_in_proj(x2, pre_g, w_main, w_gate, b_gate, *, tm=1024, tn=2048):
    m = x2.shape[0]
    q_scale = DA_QK_DIM ** -0.5 * LOG2E
    col_scale = jnp.where(jnp.arange(L0_MAIN) < DA_HEADS * 2 * DA_QK_DIM, q_scale, 1.0).astype(F32).reshape(1, -1)
    return pl.pallas_call(
        _l0_in_proj_kernel,
        grid=(m // tm, L0_MAIN // tn),
        in_specs=[
            pl.BlockSpec((tm, D_MODEL), lambda i, j: (i, 0)),
            pl.BlockSpec((1, D_MODEL), lambda i, j: (0, 0)),
            pl.BlockSpec((D_MODEL, tn), lambda i, j: (0, j)),
            pl.BlockSpec((1, tn), lambda i, j: (0, j)),
            pl.BlockSpec((D_MODEL, LANES), lambda i, j: (0, 0)),
            pl.BlockSpec((1, LANES), lambda i, j: (0, 0)),
        ],
        out_specs=[
            pl.BlockSpec((tm, tn), lambda i, j: (i, j)),
            pl.BlockSpec((8, tm), lambda i, j: (0, i)),
        ],
        out_shape=[
            jax.ShapeDtypeStruct((m, L0_MAIN), BF16),
            jax.ShapeDtypeStruct((8, m), F32),
        ],
        scratch_shapes=[pltpu.VMEM((tm, D_MODEL), BF16)],
        compiler_params=pltpu.CompilerParams(
            dimension_semantics=("arbitrary", "arbitrary"), vmem_limit_bytes=VMEM_LIMIT),
        name="l0_in_proj",
    )(x2, pre_g, w_main, col_scale, w_gate, b_gate)


def _diff_attn_kernel(vq_ref, vk_ref, lam_ref, q_ref, k_ref, v_ref, z_ref, g_ref, o_ref,
                      vt_ref, qbd_ref, acc_ref, m_ref, s0_ref, s1_ref, cmax0_ref, cmax1_ref, *, tq, lam_init):
    nq = vt_ref.shape[0]
    tk = tq
    n_off = vq_ref.shape[0]

    lam_v = lam_ref[...]
    lam = (jnp.exp(jnp.sum(lam_v[0:1] * lam_v[1:2], axis=-1, keepdims=True))
           - jnp.exp(jnp.sum(lam_v[2:3] * lam_v[3:4], axis=-1, keepdims=True)) + lam_init)

    half = tq // 2
    ones_row = (lax.broadcasted_iota(jnp.int32, (SUM_ROWS, tk), 0) == 0).astype(F32)
    first = lax.broadcasted_iota(jnp.int32, (LANES, half), 0) < DA_QK_DIM
    for i in range(nq):
        rows = slice(i * tq, (i + 1) * tq)
        vt = v_ref[0, rows, :].astype(F32).T
        vt_ref[i] = jnp.concatenate([vt, ones_row], axis=0).astype(BF16)
        qt = q_ref[0, rows, :].astype(F32).T
        parts = []
        for qh in (qt[:, 0:half], qt[:, half:tq]):
            parts += [jnp.where(first, qh, 0.0), jnp.where(first, 0.0, qh)]
        qbd_ref[i] = jnp.concatenate(parts, axis=1).astype(BF16)

    buf0, buf1 = (s0_ref, cmax0_ref), (s1_ref, cmax1_ref)

    def pipelined(n, issue, consume, issue_next_phase):
        def pair(i, issue_after):
            issue(buf1, i + 1)
            consume(buf0, i)
            issue_after()
            consume(buf1, i + 1)

        def body(t, carry):
            pair(2 * t, lambda: issue(buf0, 2 * t + 2))
            return carry

        lax.fori_loop(0, n // 2 - 1, body, 0, unroll=2)
        pair(n - 2, issue_next_phase)

    tri = (lax.broadcasted_iota(jnp.int32, (half, tq), 0)
           <= (lax.broadcasted_iota(jnp.int32, (half, tq), 1) & (half - 1)))

    def diag_scores(buf, i):
        s_ref, cmax_ref = buf
        r0 = pl.multiple_of(i * tk, tk)
        s_a = _dot(k_ref[0, pl.ds(r0, half), :], qbd_ref[i])
        s_b = _dot(k_ref[0, pl.ds(r0 + half, half), :], qbd_ref[i, :, tq:2 * tq])
        left = jnp.where(tri, s_a[:, 0:tq], NEG_BIG)
        right = s_a[:, tq:2 * tq]
        s_b = jnp.where(tri, s_b, NEG_BIG)
        s_ref[0:half, 0:tq] = left
        s_ref[0:half, tq:2 * tq] = right
        s_ref[half:tq, tq:2 * tq] = s_b
        cmax_ref[:, 0:tq] = jnp.max(left, axis=0, keepdims=True)
        cmax_ref[:, tq:2 * tq] = jnp.maximum(jnp.max(right, axis=0, keepdims=True),
                                             jnp.max(s_b, axis=0, keepdims=True))

    def diag_softmax_pv(buf, i):
        s_ref, cmax_ref = buf
        m_new = cmax_ref[...]
        p_a = jnp.exp2(s_ref[0:half, :] - m_new).astype(BF16)
        p_b = jnp.exp2(s_ref[half:tq, tq:2 * tq] - m_new[:, tq:2 * tq]).astype(BF16)
        pv_a = _dot(vt_ref[i, :, 0:half], p_a)
        pv_b = _dot(vt_ref[i, :, half:tq], p_b)
        acc_ref[i, :, 0:tq] = pv_a[:, 0:tq]
        acc_ref[i, :, tq:2 * tq] = pv_a[:, tq:2 * tq] + pv_b
        m_ref[i] = m_new

    def scores(buf, i):
        s_ref, cmax_ref = buf
        k = k_ref[0, pl.ds(pl.multiple_of(vk_ref[i] * tk, tk), tk), :]
        s = _dot(k, qbd_ref[vq_ref[i]])
        s_ref[...] = s
        cmax_ref[...] = jnp.max(s, axis=0, keepdims=True)

    def softmax_pv(buf, i):
        s_ref, cmax_ref = buf
        qi = vq_ref[i]
        m = m_ref[qi]
        m_new = jnp.maximum(m, cmax_ref[...])
        alpha = jnp.exp2(m - m_new)
        p = jnp.exp2(s_ref[...] - m_new).astype(BF16)
        acc_ref[qi] = alpha * acc_ref[qi] + _dot(vt_ref[vk_ref[i]], p)
        m_ref[qi] = m_new

    diag_scores(buf0, 0)
    pipelined(nq, diag_scores, diag_softmax_pv, lambda: scores(buf0, 0))
    pipelined(n_off, scores, softmax_pv, lambda: None)

    for i in range(nq):
        acc = acc_ref[i]
        o_t = acc[0:LANES, :] / acc[LANES:LANES + 1, :]
        o1 = jnp.concatenate([o_t[:, 0:half], o_t[:, tq:tq + half]], axis=1)
        o2 = jnp.concatenate([o_t[:, half:tq], o_t[:, tq + half:2 * tq]], axis=1)
        o = (o1 - lam * o2).T
        o = o * _rms_scale(o) * g_ref[...] * (1.0 - lam_init)
        rows = slice(i * tq, (i + 1) * tq)
        o_ref[0, rows, :] = (o * _silu(z_ref[0, rows, :].astype(F32))).astype(BF16)


def _diff_attn(p3, lam_vecs, head_g, *, tq=512):
    b, s, _ = p3.shape
    nq = s // tq
    hq, hk, hv, hz = (off // LANES for off in (_OFF_AQ, _OFF_AK, _OFF_AV, _OFF_AZ))
    lam_init = 0.8 - 0.6 * math.exp(-0.3 * 0)
    below = [(qi, kb) for qi in range(nq) for kb in range(qi)]
    assert nq % 2 == 0 and len(below) % 2 == 0
    visit_q = jnp.asarray([qi for qi, _ in below], jnp.int32)
    visit_k = jnp.asarray([kb for _, kb in below], jnp.int32)
    seq = lambda col0: pl.BlockSpec((1, s, LANES), lambda bi, h, vq, vk: (bi, 0, col0 + h))
    return pl.pallas_call(
        functools.partial(_diff_attn_kernel, tq=tq, lam_init=lam_init),
        grid_spec=pltpu.PrefetchScalarGridSpec(
            num_scalar_prefetch=2,
            grid=(b, DA_HEADS),
            in_specs=[
                pl.BlockSpec((4, DA_QK_DIM), lambda bi, h, vq, vk: (0, 0)),
                seq(hq), seq(hk), seq(hv), seq(hz),
                pl.BlockSpec((1, DA_V_DIM), lambda bi, h, vq, vk: (0, 0)),
            ],
            out_specs=pl.BlockSpec((1, s, LANES), lambda bi, h, vq, vk: (bi, 0, h)),
            scratch_shapes=[
                pltpu.VMEM((nq, LANES + SUM_ROWS, tq), BF16),
                pltpu.VMEM((nq, LANES, 2 * tq), BF16),
                pltpu.VMEM((nq, LANES + SUM_ROWS, 2 * tq), F32),
                pltpu.VMEM((nq, 1, 2 * tq), F32),
                pltpu.VMEM((tq, 2 * tq), F32),
                pltpu.VMEM((tq, 2 * tq), F32),
                pltpu.VMEM((1, 2 * tq), F32),
                pltpu.VMEM((1, 2 * tq), F32),
            ],
        ),
        out_shape=jax.ShapeDtypeStruct((b, s, DA_WIDTH), BF16),
        compiler_params=pltpu.CompilerParams(
            dimension_semantics=("arbitrary", "arbitrary"), vmem_limit_bytes=VMEM_LIMIT),
        name="diff_attn",
    )(visit_q, visit_k, lam_vecs, p3, p3, p3, p3, head_g)


def _mlstm_kernel(qp_ref, kp_ref, v_ref, o_ref, z_ref, gt_ref, cwq_ref, cwk_ref, cbq_ref, cbk_ref,
                  hg_ref, y_ref, xq_ref, xk_ref, ic_ref, bc_ref, c_ref):
    h = pl.program_id(1)
    s = qp_ref.shape[1]
    L = ML_CHUNK
    nc = s // L
    pad = 8

    zeros = jnp.zeros((pad, ML_QK_DIM), F32)
    xq_ref[0:pad, :] = zeros
    xk_ref[0:pad, :] = zeros
    xq_ref[pad:, :] = qp_ref[0].astype(F32)
    xk_ref[pad:, :] = kp_ref[0].astype(F32)

    ic = gt_ref[h]
    fp = gt_ref[ML_HEADS + h]
    lf = jnp.minimum(fp, 0.0) - jnp.log(1.0 + jnp.exp(-jnp.abs(fp)))
    r_i = lax.broadcasted_iota(jnp.int32, (L, L), 0)
    c_i = lax.broadcasted_iota(jnp.int32, (L, L), 1)
    tri_u = (r_i <= c_i).astype(BF16)
    lf_hi = lf.astype(BF16)
    lf_lo = (lf - lf_hi.astype(F32)).astype(BF16)
    ic_ref[...] = ic
    bc_ref[...] = _dot(lf_hi, tri_u) + _dot(lf_lo, tri_u)

    c_ref[...] = jnp.zeros_like(c_ref)
    causal = c_i <= r_i
    ones_col = (lax.broadcasted_iota(jnp.int32, (L, LANES), 1) == 0).astype(BF16)
    cwq = cwq_ref[...]
    cwk = cwk_ref[...]

    def conv(x_ref, r0, w, bias):
        acc = bias
        for j in range(CONV_WIDTH):
            lo = pad - (CONV_WIDTH - 1) + j
            acc = acc + w[j:j + 1, :] * x_ref[pl.ds(r0 + lo, L), :]
        return _silu(acc)

    def chunk(c, m_prev):
        r0 = pl.multiple_of(c * L, L)
        q = (conv(xq_ref, r0, cwq, cbq_ref[...]) * ML_QK_DIM ** -0.5).astype(BF16)
        k_t = conv(xk_ref, r0, cwk, cbk_ref[...]).T
        v_ext = jnp.concatenate([v_ref[0, pl.ds(r0, L), :], ones_col], axis=1)

        ic_row = ic_ref[pl.ds(c, 1), :]
        bc_row = bc_ref[pl.ds(c, 1), :]
        bc_col = jnp.broadcast_to(bc_row, (8, L)).T[:, 0:1]
        b_last = bc_row[:, L - 1:L]

        log_d = jnp.where(causal, bc_col - bc_row + ic_row, NEG_BIG)
        log_inter = bc_col + m_prev
        m_t = jnp.maximum(log_inter, jnp.max(log_d, axis=-1, keepdims=True))
        dmat = jnp.exp(log_d - m_t)
        inter_w = jnp.exp(log_inter - m_t)
        qk = (_dot(q, k_t.astype(BF16)) * dmat).astype(BF16)
        nd = _dot(qk, v_ext) + inter_w * _dot(q, c_ref[...].astype(BF16))
        num = nd[:, 0:ML_V_DIM]
        den = nd[:, ML_V_DIM:ML_V_DIM + 1]
        hm = num / jnp.maximum(jnp.abs(den), jnp.exp(-m_t))
        hm = hm * _sigmoid(o_ref[0, pl.ds(r0, L), :].astype(F32))
        hm = hm * _rms_scale(hm) * hg_ref[...]
        y_ref[0, pl.ds(r0, L), :] = (hm * _silu(z_ref[0, pl.ds(r0, L), :].astype(F32))).astype(BF16)

        g_row = b_last - bc_row + ic_row
        m_new = jnp.maximum(b_last + m_prev, jnp.max(g_row, axis=-1, keepdims=True))
        decay = jnp.exp(b_last + m_prev - m_new)
        w_row = jnp.exp(g_row - m_new)
        c_ref[...] = decay * c_ref[...] + _dot((k_t * w_row).astype(BF16), v_ext)
        return m_new

    lax.fori_loop(0, nc, chunk, jnp.zeros((1, 1), F32), unroll=8)


def _mlstm(p3, gates_t, conv_w, conv_b, head_g):
    b, s, _ = p3.shape
    nc = s // ML_CHUNK
    cq = _OFF_BQK // ML_QK_DIM
    ck = cq + ML_HEADS
    cv, co, cz = (off // ML_V_DIM for off in (5120, 6144, 7168))
    seq = lambda width, col0: pl.BlockSpec((1, s, width), lambda bi, h: (bi, 0, col0 + h))
    return pl.pallas_call(
        _mlstm_kernel,
        grid=(b, ML_HEADS),
        in_specs=[
            seq(ML_QK_DIM, cq), seq(ML_QK_DIM, ck), seq(ML_V_DIM, cv), seq(ML_V_DIM, co), seq(ML_V_DIM, cz),
            pl.BlockSpec((2 * ML_HEADS, nc, ML_CHUNK), lambda bi, h: (0, bi, 0)),
            pl.BlockSpec((CONV_WIDTH, ML_QK_DIM), lambda bi, h: (0, h)),
            pl.BlockSpec((CONV_WIDTH, ML_QK_DIM), lambda bi, h: (0, ML_HEADS + h)),
            pl.BlockSpec((1, ML_QK_DIM), lambda bi, h: (0, h)),
            pl.BlockSpec((1, ML_QK_DIM), lambda bi, h: (0, ML_HEADS + h)),
            pl.BlockSpec((1, ML_V_DIM), lambda bi, h: (0, 0)),
        ],
        out_specs=pl.BlockSpec((1, s, ML_V_DIM), lambda bi, h: (bi, 0, h)),
        out_shape=jax.ShapeDtypeStruct((b, s, ML_WIDTH), BF16),
        scratch_shapes=[
            pltpu.VMEM((s + 8, ML_QK_DIM), F32),
            pltpu.VMEM((s + 8, ML_QK_DIM), F32),
            pltpu.VMEM((nc, ML_CHUNK), F32),
            pltpu.VMEM((nc, ML_CHUNK), F32),
            pltpu.VMEM((ML_QK_DIM, ML_V_DIM + LANES), F32),
        ],
        compiler_params=pltpu.CompilerParams(
            dimension_semantics=("arbitrary", "arbitrary"), vmem_limit_bytes=VMEM_LIMIT),
        name="mlstm",
    )(p3, p3, p3, p3, p3, gates_t, conv_w, conv_w, conv_b, conv_b, head_g)


def _l0_out_proj_kernel(ya_ref, yb_ref, x_ref, wa_ref, wb_ref, g_ref, h_ref):
    y = _dot(ya_ref[...], wa_ref[...]) + _dot(yb_ref[...], wb_ref[...])
    h_ref[...] = x_ref[...] + y * _rms_scale(y) * g_ref[...]


def _l0_out_proj(ya, yb, x2, w_out, post_g, *, tm=512):
    m = x2.shape[0]
    assert DA_WIDTH == ML_WIDTH
    row = lambda width: pl.BlockSpec((tm, width), lambda i: (i, 0))
    w_rows = lambda blk: pl.BlockSpec((DA_WIDTH, D_MODEL), lambda i: (blk, 0))
    return pl.pallas_call(
        _l0_out_proj_kernel,
        grid=(m // tm,),
        in_specs=[row(DA_WIDTH), row(ML_WIDTH), row(D_MODEL),
                  w_rows(0), w_rows(1), pl.BlockSpec((1, D_MODEL), lambda i: (0, 0))],
        out_specs=row(D_MODEL),
        out_shape=jax.ShapeDtypeStruct((m, D_MODEL), F32),
        compiler_params=pltpu.CompilerParams(
            dimension_semantics=("arbitrary",), vmem_limit_bytes=VMEM_LIMIT),
        name="l0_out_proj",
    )(ya, yb, x2, w_out, w_out, post_g)


def _l1_gmlp_kernel(h_ref, pre_g_ref, wu_ref, wv_ref, wz_ref, sg_g_ref, wsp_ref, bsp_ref,
                    wo_ref, post_g_ref, o_ref, vn_ref, y_ref, *, tm):
    hres = h_ref[...]
    hn = (hres * _rms_scale(hres) * pre_g_ref[...]).astype(BF16)

    v = _gelu_tanh(_dot(hn, wv_ref[...]))
    vn_ref[...] = (v * _rms_scale(v) * sg_g_ref[...]).astype(BF16)

    r_i = lax.broadcasted_iota(jnp.int32, (SG_CHUNK, SG_CHUNK), 0)
    c_i = lax.broadcasted_iota(jnp.int32, (SG_CHUNK, SG_CHUNK), 1)
    causal = c_i <= r_i
    for g in range(SG_GROUPS):
        cols = slice(g * SG_GROUP_DIM, (g + 1) * SG_GROUP_DIM)
        u = _gelu_tanh(_dot(hn, wu_ref[:, cols]))
        gate = _silu(_dot(hn, wz_ref[:, cols]))
        wm = jnp.where(causal, wsp_ref[g], 0.0).astype(BF16)
        bias = bsp_ref[g]
        for c in range(tm // SG_CHUNK):
            rows = slice(c * SG_CHUNK, (c + 1) * SG_CHUNK)
            vs = _dot(wm, vn_ref[rows, cols]) + bias
            y_ref[rows, cols] = (u[rows] * vs * gate[rows]).astype(BF16)

    out = _dot(y_ref[...], wo_ref[...])
    o_ref[...] = hres + out * _rms_scale(out) * post_g_ref[...]


def _l1_gmlp(h1, pre_g, w_in, sg_g, w_spatial, b_spatial, wo, post_g, *, tm=512):
    m = h1.shape[0]
    row = pl.BlockSpec((tm, D_MODEL), lambda i: (i, 0))
    const2 = lambda shape: pl.BlockSpec(shape, lambda i: (0, 0), pipeline_mode=pl.Buffered(1))
    const3 = lambda shape: pl.BlockSpec(shape, lambda i: (0, 0, 0), pipeline_mode=pl.Buffered(1))
    w_cols = lambda blk: pl.BlockSpec((D_MODEL, SG_WIDTH), lambda i: (0, blk), pipeline_mode=pl.Buffered(1))
    return pl.pallas_call(
        functools.partial(_l1_gmlp_kernel, tm=tm),
        grid=(m // tm,),
        in_specs=[
            row, const2((1, D_MODEL)),
            w_cols(0), w_cols(1), w_cols(2),
            const2((1, SG_WIDTH)),
            const3((SG_GROUPS, SG_CHUNK, SG_CHUNK)), const3((SG_GROUPS, SG_CHUNK, 1)),
            const2((SG_WIDTH, D_MODEL)), const2((1, D_MODEL)),
        ],
        out_specs=row,
        out_shape=jax.ShapeDtypeStruct((m, D_MODEL), F32),
        scratch_shapes=[pltpu.VMEM((tm, SG_WIDTH), BF16), pltpu.VMEM((tm, SG_WIDTH), BF16)],
        compiler_params=pltpu.CompilerParams(
            dimension_semantics=("arbitrary",), vmem_limit_bytes=VMEM_LIMIT),
        name="l1_gmlp",
    )(h1, pre_g, w_in, w_in, w_in, sg_g, w_spatial, b_spatial, wo, post_g)


def kernel(x, l0_pre_g, l0_w_in, l0_b_igate, l0_b_fgate, l0_conv_w, l0_conv_b, l0_lambda_q1, l0_lambda_k1, l0_lambda_q2, l0_lambda_k2, l0_da_head_g, l0_ml_head_g, l0_w_out, l0_post_g, l1_pre_g, l1_w_in, l1_sg_norm_g, l1_w_spatial, l1_b_spatial, l1_w_out, l1_post_g):
    b, s, d = x.shape
    m = b * s
    x2 = x.reshape(m, d)
    row = lambda v: v.reshape(1, -1).astype(F32)

    w_main = jnp.concatenate([l0_w_in[:, :_OFF_BI], l0_w_in[:, _OFF_BO:]], axis=1).astype(BF16)
    w_gate = jnp.pad(l0_w_in[:, _OFF_BI:_OFF_BO], ((0, 0), (0, LANES - 2 * ML_HEADS))).astype(BF16)
    b_gate = jnp.pad(jnp.concatenate([l0_b_igate, l0_b_fgate]), (0, LANES - 2 * ML_HEADS)).reshape(1, LANES)
    lam_vecs = jnp.stack([l0_lambda_q1, l0_lambda_k1, l0_lambda_q2, l0_lambda_k2]).astype(F32)

    p, gates_t = _l0_in_proj(x2, row(l0_pre_g), w_main, w_gate, b_gate.astype(F32))
    p3 = p.reshape(b, s, L0_MAIN)
    gates_t = gates_t.reshape(2 * ML_HEADS, m // ML_CHUNK, ML_CHUNK)

    y_a = _diff_attn(p3, lam_vecs, row(l0_da_head_g))
    y_b = _mlstm(p3, gates_t, l0_conv_w.astype(F32), row(l0_conv_b), row(l0_ml_head_g))

    h1 = _l0_out_proj(y_a.reshape(m, DA_WIDTH), y_b.reshape(m, ML_WIDTH), x2,
                      l0_w_out.astype(BF16), row(l0_post_g))

    h2 = _l1_gmlp(h1, row(l1_pre_g), l1_w_in.astype(BF16),
                  row(l1_sg_norm_g), l1_w_spatial.astype(F32),
                  l1_b_spatial.astype(F32).reshape(SG_GROUPS, SG_CHUNK, 1),
                  l1_w_out.astype(BF16), row(l1_post_g))
    return h2.reshape(b, s, d)
```

```python
import functools
import math

import jax
import jax.numpy as jnp
from jax import lax
from jax.experimental import pallas as pl
from jax.experimental.pallas import tpu as pltpu

F32 = jnp.float32
BF16 = jnp.bfloat16

D_MODEL = 1024
EPS = 1e-6
DA_HEADS = 8
DA_QK_DIM = 64
DA_V_DIM = 128
DA_WIDTH = DA_HEADS * DA_V_DIM
ML_HEADS = 4
ML_QK_DIM = 128
ML_V_DIM = 256
ML_WIDTH = ML_HEADS * ML_V_DIM
ML_CHUNK = 128
CONV_WIDTH = 4
SG_GROUPS = 8
SG_CHUNK = 128
SG_WIDTH = 2 * D_MODEL
SG_GROUP_DIM = SG_WIDTH // SG_GROUPS

LANES = 128
SUM_ROWS = 16
VMEM_LIMIT = 56 * 1024 * 1024
NEG_BIG = -1e30
LOG2E = 1.4426950408889634

_OFF_AQ, _OFF_AK, _OFF_AV, _OFF_AZ = 0, 1024, 2048, 3072
_OFF_BQK, _OFF_BV, _OFF_BI, _OFF_BF, _OFF_BO, _OFF_BZ = 4096, 5120, 6144, 6148, 6152, 7176
L0_MAIN = 8192


def _sigmoid(x):
    return 1.0 / (1.0 + jnp.exp(-x))


def _silu(x):
    return x * _sigmoid(x)


def _gelu_tanh(x):
    c = math.sqrt(2.0 / math.pi)
    return 0.5 * x * (1.0 + jnp.tanh(c * (x + 0.044715 * (x * x * x))))


def _rms_scale(x):
    return lax.rsqrt(jnp.mean(x * x, axis=-1, keepdims=True) + EPS)


def _dot(a, b):
    return jnp.dot(a, b, preferred_element_type=F32)


def _l0_in_proj_kernel(x_ref, g_ref, w_ref, cs_ref, wg_ref, bg_ref, p_ref, gt_ref, hn_ref):
    j = pl.program_id(1)

    @pl.when(j == 0)
    def _():
        x = x_ref[...]
        hn = (x * _rms_scale(x) * g_ref[...]).astype(BF16)
        hn_ref[...] = hn
        gates = _dot(hn, wg_ref[...]) + bg_ref[...]
        gt_ref[...] = gates.T[0:8, :]

    p_ref[...] = (_dot(hn_ref[...], w_ref[...]) * cs_ref[...]).astype(BF16)


def _l0_in_proj(x2, pre_g, w_main, w_gate, b_gate, *, tm=2048, tn=2048):
    m = x2.shape[0]
    q_scale = DA_QK_DIM ** -0.5 * LOG2E
    col_scale = jnp.where(jnp.arange(L0_MAIN) < DA_HEADS * 2 * DA_QK_DIM, q_scale, 1.0).astype(F32).reshape(1, -1)
    return pl.pallas_call(
        _l0_in_proj_kernel,
        grid=(m // tm, L0_MAIN // tn),
        in_specs=[
            pl.BlockSpec((tm, D_MODEL), lambda i, j: (i, 0)),
            pl.BlockSpec((1, D_MODEL), lambda i, j: (0, 0)),
            pl.BlockSpec((D_MODEL, tn), lambda i, j: (0, j)),
            pl.BlockSpec((1, tn), lambda i, j: (0, j)),
            pl.BlockSpec((D_MODEL, LANES), lambda i, j: (0, 0)),
            pl.BlockSpec((1, LANES), lambda i, j: (0, 0)),
        ],
        out_specs=[
            pl.BlockSpec((tm, tn), lambda i, j: (i, j)),
            pl.BlockSpec((8, tm), lambda i, j: (0, i)),
        ],
        out_shape=[
            jax.ShapeDtypeStruct((m, L0_MAIN), BF16),
            jax.ShapeDtypeStruct((8, m), F32),
        ],
        scratch_shapes=[pltpu.VMEM((tm, D_MODEL), BF16)],
        compiler_params=pltpu.CompilerParams(
            dimension_semantics=("arbitrary", "arbitrary"), vmem_limit_bytes=VMEM_LIMIT),
        name="l0_in_proj",
    )(x2, pre_g, w_main, col_scale, w_gate, b_gate)


def _diff_attn_kernel(vq_ref, vk_ref, lam_ref, q_ref, k_ref, v_ref, z_ref, g_ref, o_ref,
                      vt_ref, qbd_ref, acc_ref, m_ref, s0_ref, s1_ref, cmax0_ref, cmax1_ref, *, tq, lam_init):
    nq = vt_ref.shape[0]
    tk = tq
    n_off = vq_ref.shape[0]

    lam_v = lam_ref[...]
    lam = (jnp.exp(jnp.sum(lam_v[0:1] * lam_v[1:2], axis=-1, keepdims=True))
           - jnp.exp(jnp.sum(lam_v[2:3] * lam_v[3:4], axis=-1, keepdims=True)) + lam_init)

    half = tq // 2
    ones_row = (lax.broadcasted_iota(jnp.int32, (SUM_ROWS, tk), 0) == 0).astype(F32)
    first = lax.broadcasted_iota(jnp.int32, (LANES, half), 0) < DA_QK_DIM
    for i in range(nq):
        rows = slice(i * tq, (i + 1) * tq)
        vt = v_ref[0, rows, :].astype(F32).T
        vt_ref[i] = jnp.concatenate([vt, ones_row], axis=0).astype(BF16)
        qt = q_ref[0, rows, :].astype(F32).T
        parts = []
        for qh in (qt[:, 0:half], qt[:, half:tq]):
            parts += [jnp.where(first, qh, 0.0), jnp.where(first, 0.0, qh)]
        qbd_ref[i] = jnp.concatenate(parts, axis=1).astype(BF16)

    buf0, buf1 = (s0_ref, cmax0_ref), (s1_ref, cmax1_ref)

    def pipelined(n, issue, consume, issue_next_phase):
        def pair(i, issue_after):
            issue(buf1, i + 1)
            consume(buf0, i)
            issue_after()
            consume(buf1, i + 1)

        def body(t, carry):
            pair(2 * t, lambda: issue(buf0, 2 * t + 2))
            return carry

        lax.fori_loop(0, n // 2 - 1, body, 0, unroll=2)
        pair(n - 2, issue_next_phase)

    tri = (lax.broadcasted_iota(jnp.int32, (half, tq), 0)
           <= (lax.broadcasted_iota(jnp.int32, (half, tq), 1) & (half - 1)))

    def diag_scores(buf, i):
        s_ref, cmax_ref = buf
        r0 = pl.multiple_of(i * tk, tk)
        s_a = _dot(k_ref[0, pl.ds(r0, half), :], qbd_ref[i])
        s_b = _dot(k_ref[0, pl.ds(r0 + half, half), :], qbd_ref[i, :, tq:2 * tq])
        left = jnp.where(tri, s_a[:, 0:tq], NEG_BIG)
        right = s_a[:, tq:2 * tq]
        s_b = jnp.where(tri, s_b, NEG_BIG)
        s_ref[0:half, 0:tq] = left
        s_ref[0:half, tq:2 * tq] = right
        s_ref[half:tq, tq:2 * tq] = s_b
        cmax_ref[:, 0:tq] = jnp.max(left, axis=0, keepdims=True)
        cmax_ref[:, tq:2 * tq] = jnp.maximum(jnp.max(right, axis=0, keepdims=True),
                                             jnp.max(s_b, axis=0, keepdims=True))

    def diag_softmax_pv(buf, i):
        s_ref, cmax_ref = buf
        m_new = cmax_ref[...]
        p_a = jnp.exp2(s_ref[0:half, :] - m_new).astype(BF16)
        p_b = jnp.exp2(s_ref[half:tq, tq:2 * tq] - m_new[:, tq:2 * tq]).astype(BF16)
        pv_a = _dot(vt_ref[i, :, 0:half], p_a)
        pv_b = _dot(vt_ref[i, :, half:tq], p_b)
        acc_ref[i, :, 0:tq] = pv_a[:, 0:tq]
        acc_ref[i, :, tq:2 * tq] = pv_a[:, tq:2 * tq] + pv_b
        m_ref[i] = m_new

    def scores(buf, i):
        s_ref, cmax_ref = buf
        k = k_ref[0, pl.ds(pl.multiple_of(vk_ref[i] * tk, tk), tk), :]
        s = _dot(k, qbd_ref[vq_ref[i]])
        s_ref[...] = s
        cmax_ref[...] = jnp.max(s, axis=0, keepdims=True)

    def softmax_pv(buf, i):
        s_ref, cmax_ref = buf
        qi = vq_ref[i]
        m = m_ref[qi]
        m_new = jnp.maximum(m, cmax_ref[...])
        alpha = jnp.exp2(m - m_new)
        p = jnp.exp2(s_ref[...] - m_new).astype(BF16)
        acc_ref[qi] = alpha * acc_ref[qi] + _dot(vt_ref[vk_ref[i]], p)
        m_ref[qi] = m_new

    diag_scores(buf0, 0)
    pipelined(nq, diag_scores, diag_softmax_pv, lambda: scores(buf0, 0))
    pipelined(n_off, scores, softmax_pv, lambda: None)

    for i in range(nq):
        acc = acc_ref[i]
        o_t = acc[0:LANES, :] / acc[LANES:LANES + 1, :]
        o1 = jnp.concatenate([o_t[:, 0:half], o_t[:, tq:tq + half]], axis=1)
        o2 = jnp.concatenate([o_t[:, half:tq], o_t[:, tq + half:2 * tq]], axis=1)
        o = (o1 - lam * o2).T
        o = o * _rms_scale(o) * g_ref[...] * (1.0 - lam_init)
        rows = slice(i * tq, (i + 1) * tq)
        o_ref[0, rows, :] = (o * _silu(z_ref[0, rows, :].astype(F32))).astype(BF16)


def _diff_attn(p3, lam_vecs, head_g, *, tq=512):
    b, s, _ = p3.shape
    nq = s // tq
    hq, hk, hv, hz = (off // LANES for off in (_OFF_AQ, _OFF_AK, _OFF_AV, _OFF_AZ))
    lam_init = 0.8 - 0.6 * math.exp(-0.3 * 0)
    below = [(qi, kb) for qi in range(nq) for kb in range(qi)]
    assert nq % 2 == 0 and len(below) % 2 == 0
    visit_q = jnp.asarray([qi for qi, _ in below], jnp.int32)
    visit_k = jnp.asarray([kb for _, kb in below], jnp.int32)
    seq = lambda col0: pl.BlockSpec((1, s, LANES), lambda bi, h, vq, vk: (bi, 0, col0 + h))
    return pl.pallas_call(
        functools.partial(_diff_attn_kernel, tq=tq, lam_init=lam_init),
        grid_spec=pltpu.PrefetchScalarGridSpec(
            num_scalar_prefetch=2,
            grid=(b, DA_HEADS),
            in_specs=[
                pl.BlockSpec((4, DA_QK_DIM), lambda bi, h, vq, vk: (0, 0)),
                seq(hq), seq(hk), seq(hv), seq(hz),
                pl.BlockSpec((1, DA_V_DIM), lambda bi, h, vq, vk: (0, 0)),
            ],
            out_specs=pl.BlockSpec((1, s, LANES), lambda bi, h, vq, vk: (bi, 0, h)),
            scratch_shapes=[
                pltpu.VMEM((nq, LANES + SUM_ROWS, tq), BF16),
                pltpu.VMEM((nq, LANES, 2 * tq), BF16),
                pltpu.VMEM((nq, LANES + SUM_ROWS, 2 * tq), F32),
                pltpu.VMEM((nq, 1, 2 * tq), F32),
                pltpu.VMEM((tq, 2 * tq), F32),
                pltpu.VMEM((tq, 2 * tq), F32),
                pltpu.VMEM((1, 2 * tq), F32),
                pltpu.VMEM((1, 2 * tq), F32),
            ],
        ),
        out_shape=jax.ShapeDtypeStruct((b, s, DA_WIDTH), BF16),
        compiler_params=pltpu.CompilerParams(
            dimension_semantics=("arbitrary", "arbitrary"), vmem_limit_bytes=VMEM_LIMIT),
        name="diff_attn",
    )(visit_q, visit_k, lam_vecs, p3, p3, p3, p3, head_g)


def _mlstm_kernel(qp_ref, kp_ref, v_ref, o_ref, z_ref, gt_ref, cwq_ref, cwk_ref, cbq_ref, cbk_ref,
                  hg_ref, y_ref, xq_ref, xk_ref, ic_ref, bc_ref, c_ref):
    h = pl.program_id(1)
    s = qp_ref.shape[1]
    L = ML_CHUNK
    nc = s // L
    pad = 8

    zeros = jnp.zeros((pad, ML_QK_DIM), F32)
    xq_ref[0:pad, :] = zeros
    xk_ref[0:pad, :] = zeros
    xq_ref[pad:, :] = qp_ref[0].astype(F32)
    xk_ref[pad:, :] = kp_ref[0].astype(F32)

    ic = gt_ref[h]
    fp = gt_ref[ML_HEADS + h]
    lf = jnp.minimum(fp, 0.0) - jnp.log(1.0 + jnp.exp(-jnp.abs(fp)))
    r_i = lax.broadcasted_iota(jnp.int32, (L, L), 0)
    c_i = lax.broadcasted_iota(jnp.int32, (L, L), 1)
    tri_u = (r_i <= c_i).astype(BF16)
    lf_hi = lf.astype(BF16)
    lf_lo = (lf - lf_hi.astype(F32)).astype(BF16)
    ic_ref[...] = ic
    bc_ref[...] = _dot(lf_hi, tri_u) + _dot(lf_lo, tri_u)

    c_ref[...] = jnp.zeros_like(c_ref)
    causal = c_i <= r_i
    ones_col = (lax.broadcasted_iota(jnp.int32, (L, LANES), 1) == 0).astype(BF16)
    cwq = cwq_ref[...]
    cwk = cwk_ref[...]

    def conv(x_ref, r0, w, bias):
        xw = x_ref[pl.ds(r0, L + pad), :]
        acc = bias
        for j in range(CONV_WIDTH):
            lo = pad - (CONV_WIDTH - 1) + j
            acc = acc + w[j:j + 1, :] * xw[lo:lo + L, :]
        return _silu(acc)

    def chunk(c, m_prev):
        r0 = pl.multiple_of(c * L, L)
        q = (conv(xq_ref, r0, cwq, cbq_ref[...]) * ML_QK_DIM ** -0.5).astype(BF16)
        k_t = conv(xk_ref, r0, cwk, cbk_ref[...]).T
        v_ext = jnp.concatenate([v_ref[0, pl.ds(r0, L), :], ones_col], axis=1)

        ic_row = ic_ref[pl.ds(c, 1), :]
        bc_row = bc_ref[pl.ds(c, 1), :]
        bc_col = jnp.broadcast_to(bc_row, (8, L)).T[:, 0:1]
        b_last = bc_row[:, L - 1:L]

        log_d = jnp.where(causal, bc_col - bc_row + ic_row, NEG_BIG)
        log_inter = bc_col + m_prev
        m_t = jnp.maximum(log_inter, jnp.max(log_d, axis=-1, keepdims=True))
        dmat = jnp.exp(log_d - m_t)
        inter_w = jnp.exp(log_inter - m_t)
        qk = (_dot(q, k_t.astype(BF16)) * dmat).astype(BF16)
        nd = _dot(qk, v_ext) + inter_w * _dot(q, c_ref[...].astype(BF16))
        num = nd[:, 0:ML_V_DIM]
        den = nd[:, ML_V_DIM:ML_V_DIM + 1]
        hm = num / jnp.maximum(jnp.abs(den), jnp.exp(-m_t))
        hm = hm * _sigmoid(o_ref[0, pl.ds(r0, L), :].astype(F32))
        hm = hm * _rms_scale(hm) * hg_ref[...]
        y_ref[0, pl.ds(r0, L), :] = (hm * _silu(z_ref[0, pl.ds(r0, L), :].astype(F32))).astype(BF16)

        g_row = b_last - bc_row + ic_row
        m_new = jnp.maximum(b_last + m_prev, jnp.max(g_row, axis=-1, keepdims=True))
        decay = jnp.exp(b_last + m_prev - m_new)
        w_row = jnp.exp(g_row - m_new)
        c_ref[...] = decay * c_ref[...] + _dot((k_t * w_row).astype(BF16), v_ext)
        return m_new

    lax.fori_loop(0, nc, chunk, jnp.zeros((1, 1), F32), unroll=8)


def _mlstm(p3, gates_t, conv_w, conv_b, head_g):
    b, s, _ = p3.shape
    nc = s // ML_CHUNK
    cq = _OFF_BQK // ML_QK_DIM
    ck = cq + ML_HEADS
    cv, co, cz = (off // ML_V_DIM for off in (5120, 6144, 7168))
    seq = lambda width, col0: pl.BlockSpec((1, s, width), lambda bi, h: (bi, 0, col0 + h))
    return pl.pallas_call(
        _mlstm_kernel,
        grid=(b, ML_HEADS),
        in_specs=[
            seq(ML_QK_DIM, cq), seq(ML_QK_DIM, ck), seq(ML_V_DIM, cv), seq(ML_V_DIM, co), seq(ML_V_DIM, cz),
            pl.BlockSpec((2 * ML_HEADS, nc, ML_CHUNK), lambda bi, h: (0, bi, 0)),
            pl.BlockSpec((CONV_WIDTH, ML_QK_DIM), lambda bi, h: (0, h)),
            pl.BlockSpec((CONV_WIDTH, ML_QK_DIM), lambda bi, h: (0, ML_HEADS + h)),
            pl.BlockSpec((1, ML_QK_DIM), lambda bi, h: (0, h)),
            pl.BlockSpec((1, ML_QK_DIM), lambda bi, h: (0, ML_HEADS + h)),
            pl.BlockSpec((1, ML_V_DIM), lambda bi, h: (0, 0)),
        ],
        out_specs=pl.BlockSpec((1, s, ML_V_DIM), lambda bi, h: (bi, 0, h)),
        out_shape=jax.ShapeDtypeStruct((b, s, ML_WIDTH), BF16),
        scratch_shapes=[
            pltpu.VMEM((s + 8, ML_QK_DIM), F32),
            pltpu.VMEM((s + 8, ML_QK_DIM), F32),
            pltpu.VMEM((nc, ML_CHUNK), F32),
            pltpu.VMEM((nc, ML_CHUNK), F32),
            pltpu.VMEM((ML_QK_DIM, ML_V_DIM + LANES), F32),
        ],
        compiler_params=pltpu.CompilerParams(
            dimension_semantics=("arbitrary", "arbitrary"), vmem_limit_bytes=VMEM_LIMIT),
        name="mlstm",
    )(p3, p3, p3, p3, p3, gates_t, conv_w, conv_w, conv_b, conv_b, head_g)


def _l0_out_proj_kernel(ya_ref, yb_ref, x_ref, wa_ref, wb_ref, g_ref, h_ref):
    y = _dot(ya_ref[...], wa_ref[...]) + _dot(yb_ref[...], wb_ref[...])
    h_ref[...] = x_ref[...] + y * _rms_scale(y) * g_ref[...]


def _l0_out_proj(ya, yb, x2, w_out, post_g, *, tm=512):
    m = x2.shape[0]
    assert DA_WIDTH == ML_WIDTH
    row = lambda width: pl.BlockSpec((tm, width), lambda i: (i, 0))
    w_rows = lambda blk: pl.BlockSpec((DA_WIDTH, D_MODEL), lambda i: (blk, 0))
    return pl.pallas_call(
        _l0_out_proj_kernel,
        grid=(m // tm,),
        in_specs=[row(DA_WIDTH), row(ML_WIDTH), row(D_MODEL),
                  w_rows(0), w_rows(1), pl.BlockSpec((1, D_MODEL), lambda i: (0, 0))],
        out_specs=row(D_MODEL),
        out_shape=jax.ShapeDtypeStruct((m, D_MODEL), F32),
        compiler_params=pltpu.CompilerParams(
            dimension_semantics=("arbitrary",), vmem_limit_bytes=VMEM_LIMIT),
        name="l0_out_proj",
    )(ya, yb, x2, w_out, w_out, post_g)


def _l1_gmlp_kernel(h_ref, pre_g_ref, wu_ref, wv_ref, wz_ref, sg_g_ref, wsp_ref, bsp_ref,
                    wo_ref, post_g_ref, o_ref, vn_ref, y_ref, *, tm):
    hres = h_ref[...]
    hn = (hres * _rms_scale(hres) * pre_g_ref[...]).astype(BF16)

    v = _gelu_tanh(_dot(hn, wv_ref[...]))
    vn_ref[...] = (v * _rms_scale(v) * sg_g_ref[...]).astype(BF16)

    r_i = lax.broadcasted_iota(jnp.int32, (SG_CHUNK, SG_CHUNK), 0)
    c_i = lax.broadcasted_iota(jnp.int32, (SG_CHUNK, SG_CHUNK), 1)
    causal = c_i <= r_i
    for g in range(SG_GROUPS):
        cols = slice(g * SG_GROUP_DIM, (g + 1) * SG_GROUP_DIM)
        u = _gelu_tanh(_dot(hn, wu_ref[:, cols]))
        gate = _silu(_dot(hn, wz_ref[:, cols]))
        wm = jnp.where(causal, wsp_ref[g], 0.0).astype(BF16)
        bias = bsp_ref[g]
        for c in range(tm // SG_CHUNK):
            rows = slice(c * SG_CHUNK, (c + 1) * SG_CHUNK)
            vs = _dot(wm, vn_ref[rows, cols]) + bias
            y_ref[rows, cols] = (u[rows] * vs * gate[rows]).astype(BF16)

    out = _dot(y_ref[...], wo_ref[...])
    o_ref[...] = hres + out * _rms_scale(out) * post_g_ref[...]


def _l1_gmlp(h1, pre_g, w_in, sg_g, w_spatial, b_spatial, wo, post_g, *, tm=512):
    m = h1.shape[0]
    row = pl.BlockSpec((tm, D_MODEL), lambda i: (i, 0))
    const2 = lambda shape: pl.BlockSpec(shape, lambda i: (0, 0), pipeline_mode=pl.Buffered(1))
    const3 = lambda shape: pl.BlockSpec(shape, lambda i: (0, 0, 0), pipeline_mode=pl.Buffered(1))
    w_cols = lambda blk: pl.BlockSpec((D_MODEL, SG_WIDTH), lambda i: (0, blk), pipeline_mode=pl.Buffered(1))
    return pl.pallas_call(
        functools.partial(_l1_gmlp_kernel, tm=tm),
        grid=(m // tm,),
        in_specs=[
            row, const2((1, D_MODEL)),
            w_cols(0), w_cols(1), w_cols(2),
            const2((1, SG_WIDTH)),
            const3((SG_GROUPS, SG_CHUNK, SG_CHUNK)), const3((SG_GROUPS, SG_CHUNK, 1)),
            const2((SG_WIDTH, D_MODEL)), const2((1, D_MODEL)),
        ],
        out_specs=row,
        out_shape=jax.ShapeDtypeStruct((m, D_MODEL), F32),
        scratch_shapes=[pltpu.VMEM((tm, SG_WIDTH), BF16), pltpu.VMEM((tm, SG_WIDTH), BF16)],
        compiler_params=pltpu.CompilerParams(
            dimension_semantics=("arbitrary",), vmem_limit_bytes=VMEM_LIMIT),
        name="l1_gmlp",
    )(h1, pre_g, w_in, w_in, w_in, sg_g, w_spatial, b_spatial, wo, post_g)


def kernel(x, l0_pre_g, l0_w_in, l0_b_igate, l0_b_fgate, l0_conv_w, l0_conv_b, l0_lambda_q1, l0_lambda_k1, l0_lambda_q2, l0_lambda_k2, l0_da_head_g, l0_ml_head_g, l0_w_out, l0_post_g, l1_pre_g, l1_w_in, l1_sg_norm_g, l1_w_spatial, l1_b_spatial, l1_w_out, l1_post_g):
    b, s, d = x.shape
    m = b * s
    x2 = x.reshape(m, d)
    row = lambda v: v.reshape(1, -1).astype(F32)

    w_in0 = l0_w_in.astype(BF16)
    w_main = jnp.concatenate([w_in0[:, :_OFF_BI], w_in0[:, _OFF_BO:]], axis=1)
    w_gate = jnp.pad(w_in0[:, _OFF_BI:_OFF_BO], ((0, 0), (0, LANES - 2 * ML_HEADS)))
    b_gate = jnp.pad(jnp.concatenate([l0_b_igate, l0_b_fgate]), (0, LANES - 2 * ML_HEADS)).reshape(1, LANES)
    lam_vecs = jnp.stack([l0_lambda_q1, l0_lambda_k1, l0_lambda_q2, l0_lambda_k2]).astype(F32)

    p, gates_t = _l0_in_proj(x2, row(l0_pre_g), w_main, w_gate, b_gate.astype(F32))
    p3 = p.reshape(b, s, L0_MAIN)
    gates_t = gates_t.reshape(2 * ML_HEADS, m // ML_CHUNK, ML_CHUNK)

    y_a = _diff_attn(p3, lam_vecs, row(l0_da_head_g))
    y_b = _mlstm(p3, gates_t, l0_conv_w.astype(F32), row(l0_conv_b), row(l0_ml_head_g))

    h1 = _l0_out_proj(y_a.reshape(m, DA_WIDTH), y_b.reshape(m, ML_WIDTH), x2,
                      l0_w_out.astype(BF16), row(l0_post_g))

    h2 = _l1_gmlp(h1, row(l1_pre_g), l1_w_in.astype(BF16),
                  row(l1_sg_norm_g), l1_w_spatial.astype(F32),
                  l1_b_spatial.astype(F32).reshape(SG_GROUPS, SG_CHUNK, 1),
                  l1_w_out.astype(BF16), row(l1_post_g))
    return h2.reshape(b, s, d)
```

```python
import functools
import math

import jax
import jax.numpy as jnp
from jax import lax
from jax.experimental import pallas as pl
from jax.experimental.pallas import tpu as pltpu

F32 = jnp.float32
BF16 = jnp.bfloat16

D_MODEL = 1024
EPS = 1e-6
DA_HEADS = 8
DA_QK_DIM = 64
DA_V_DIM = 128
DA_WIDTH = DA_HEADS * DA_V_DIM
ML_HEADS = 4
ML_QK_DIM = 128
ML_V_DIM = 256
ML_WIDTH = ML_HEADS * ML_V_DIM
ML_CHUNK = 128
CONV_WIDTH = 4
SG_GROUPS = 8
SG_CHUNK = 128
SG_WIDTH = 2 * D_MODEL
SG_GROUP_DIM = SG_WIDTH // SG_GROUPS

LANES = 128
SUM_ROWS = 16
VMEM_LIMIT = 56 * 1024 * 1024
NEG_BIG = -1e30
LOG2E = 1.4426950408889634

_OFF_AQ, _OFF_AK, _OFF_AV, _OFF_AZ = 0, 1024, 2048, 3072
_OFF_BQK, _OFF_BV, _OFF_BI, _OFF_BF, _OFF_BO, _OFF_BZ = 4096, 5120, 6144, 6148, 6152, 7176
L0_MAIN = 8192


def _sigmoid(x):
    return 1.0 / (1.0 + jnp.exp(-x))


def _silu(x):
    return x * _sigmoid(x)


def _gelu_tanh(x):
    c = math.sqrt(2.0 / math.pi)
    return 0.5 * x * (1.0 + jnp.tanh(c * (x + 0.044715 * (x * x * x))))


def _rms_scale(x):
    return lax.rsqrt(jnp.mean(x * x, axis=-1, keepdims=True) + EPS)


def _dot(a, b):
    return jnp.dot(a, b, preferred_element_type=F32)


def _l0_in_proj_kernel(x_ref, g_ref, w_ref, cs_ref, wg_ref, bg_ref, p_ref, gt_ref, hn_ref):
    j = pl.program_id(1)

    @pl.when(j == 0)
    def _():
        x = x_ref[...]
        hn = (x * _rms_scale(x) * g_ref[...]).astype(BF16)
        hn_ref[...] = hn
        gates = _dot(hn, wg_ref[...]) + bg_ref[...]
        gt_ref[...] = gates.T[0:8, :]

    p_ref[...] = (_dot(hn_ref[...], w_ref[...]) * cs_ref[...]).astype(BF16)


def _l0_in_proj(x2, pre_g, w_main, w_gate, b_gate, *, tm=2048, tn=2048):
    m = x2.shape[0]
    q_scale = DA_QK_DIM ** -0.5 * LOG2E
    col_scale = jnp.where(jnp.arange(L0_MAIN) < DA_HEADS * 2 * DA_QK_DIM, q_scale, 1.0).astype(F32).reshape(1, -1)
    return pl.pallas_call(
        _l0_in_proj_kernel,
        grid=(m // tm, L0_MAIN // tn),
        in_specs=[
            pl.BlockSpec((tm, D_MODEL), lambda i, j: (i, 0)),
            pl.BlockSpec((1, D_MODEL), lambda i, j: (0, 0)),
            pl.BlockSpec((D_MODEL, tn), lambda i, j: (0, j)),
            pl.BlockSpec((1, tn), lambda i, j: (0, j)),
            pl.BlockSpec((D_MODEL, LANES), lambda i, j: (0, 0)),
            pl.BlockSpec((1, LANES), lambda i, j: (0, 0)),
        ],
        out_specs=[
            pl.BlockSpec((tm, tn), lambda i, j: (i, j)),
            pl.BlockSpec((8, tm), lambda i, j: (0, i)),
        ],
        out_shape=[
            jax.ShapeDtypeStruct((m, L0_MAIN), BF16),
            jax.ShapeDtypeStruct((8, m), F32),
        ],
        scratch_shapes=[pltpu.VMEM((tm, D_MODEL), BF16)],
        compiler_params=pltpu.CompilerParams(
            dimension_semantics=("arbitrary", "arbitrary"), vmem_limit_bytes=VMEM_LIMIT),
        name="l0_in_proj",
    )(x2, pre_g, w_main, col_scale, w_gate, b_gate)


def _diff_attn_kernel(vq_ref, vk_ref, lam_ref, q_ref, k_ref, v_ref, z_ref, g_ref, o_ref,
                      vt_ref, qbd_ref, acc_ref, m_ref, s0_ref, s1_ref, cmax0_ref, cmax1_ref, *, tq, lam_init):
    nq = vt_ref.shape[0]
    tk = tq
    n_off = vq_ref.shape[0]

    lam_v = lam_ref[...]
    lam = (jnp.exp(jnp.sum(lam_v[0:1] * lam_v[1:2], axis=-1, keepdims=True))
           - jnp.exp(jnp.sum(lam_v[2:3] * lam_v[3:4], axis=-1, keepdims=True)) + lam_init)

    half = tq // 2
    ones_row = (lax.broadcasted_iota(jnp.int32, (SUM_ROWS, tk), 0) == 0).astype(F32)
    first = lax.broadcasted_iota(jnp.int32, (LANES, half), 0) < DA_QK_DIM
    for i in range(nq):
        rows = slice(i * tq, (i + 1) * tq)
        vt = v_ref[0, rows, :].astype(F32).T
        vt_ref[i] = jnp.concatenate([vt, ones_row], axis=0).astype(BF16)
        qt = q_ref[0, rows, :].astype(F32).T
        parts = []
        for qh in (qt[:, 0:half], qt[:, half:tq]):
            parts += [jnp.where(first, qh, 0.0), jnp.where(first, 0.0, qh)]
        qbd_ref[i] = jnp.concatenate(parts, axis=1).astype(BF16)

    buf0, buf1 = (s0_ref, cmax0_ref), (s1_ref, cmax1_ref)

    def pipelined(n, issue, consume, issue_next_phase):
        def pair(i, issue_after):
            issue(buf1, i + 1)
            consume(buf0, i)
            issue_after()
            consume(buf1, i + 1)

        def body(t, carry):
            pair(2 * t, lambda: issue(buf0, 2 * t + 2))
            return carry

        lax.fori_loop(0, n // 2 - 1, body, 0, unroll=2)
        pair(n - 2, issue_next_phase)

    tri = (lax.broadcasted_iota(jnp.int32, (half, tq), 0)
           <= (lax.broadcasted_iota(jnp.int32, (half, tq), 1) & (half - 1)))

    def diag_scores(buf, i):
        s_ref, cmax_ref = buf
        r0 = pl.multiple_of(i * tk, tk)
        s_a = _dot(k_ref[0, pl.ds(r0, half), :], qbd_ref[i])
        s_b = _dot(k_ref[0, pl.ds(r0 + half, half), :], qbd_ref[i, :, tq:2 * tq])
        left = jnp.where(tri, s_a[:, 0:tq], NEG_BIG)
        right = s_a[:, tq:2 * tq]
        s_b = jnp.where(tri, s_b, NEG_BIG)
        s_ref[0:half, 0:tq] = left
        s_ref[0:half, tq:2 * tq] = right
        s_ref[half:tq, tq:2 * tq] = s_b
        cmax_ref[:, 0:tq] = jnp.max(left, axis=0, keepdims=True)
        cmax_ref[:, tq:2 * tq] = jnp.maximum(jnp.max(right, axis=0, keepdims=True),
                                             jnp.max(s_b, axis=0, keepdims=True))

    def diag_softmax_pv(buf, i):
        s_ref, cmax_ref = buf
        m_new = cmax_ref[...]
        p_a = jnp.exp2(s_ref[0:half, :] - m_new).astype(BF16)
        p_b = jnp.exp2(s_ref[half:tq, tq:2 * tq] - m_new[:, tq:2 * tq]).astype(BF16)
        pv_a = _dot(vt_ref[i, :, 0:half], p_a)
        pv_b = _dot(vt_ref[i, :, half:tq], p_b)
        acc_ref[i, :, 0:tq] = pv_a[:, 0:tq]
        acc_ref[i, :, tq:2 * tq] = pv_a[:, tq:2 * tq] + pv_b
        m_ref[i] = m_new

    def scores(buf, i):
        s_ref, cmax_ref = buf
        k = k_ref[0, pl.ds(pl.multiple_of(vk_ref[i] * tk, tk), tk), :]
        s = _dot(k, qbd_ref[vq_ref[i]])
        s_ref[...] = s
        cmax_ref[...] = jnp.max(s, axis=0, keepdims=True)

    def softmax_pv(buf, i):
        s_ref, cmax_ref = buf
        qi = vq_ref[i]
        m = m_ref[qi]
        m_new = jnp.maximum(m, cmax_ref[...])
        alpha = jnp.exp2(m - m_new)
        p = jnp.exp2(s_ref[...] - m_new).astype(BF16)
        acc_ref[qi] = alpha * acc_ref[qi] + _dot(vt_ref[vk_ref[i]], p)
        m_ref[qi] = m_new

    diag_scores(buf0, 0)
    pipelined(nq, diag_scores, diag_softmax_pv, lambda: scores(buf0, 0))
    pipelined(n_off, scores, softmax_pv, lambda: None)

    gain_col = jnp.broadcast_to(g_ref[...] * (1.0 - lam_init), (8, LANES)).T[:, 0:1]
    for i in range(nq):
        acc = acc_ref[i]
        o_t = acc[0:LANES, :] / acc[LANES:LANES + 1, :]
        o1 = jnp.concatenate([o_t[:, 0:half], o_t[:, tq:tq + half]], axis=1)
        o2 = jnp.concatenate([o_t[:, half:tq], o_t[:, tq + half:2 * tq]], axis=1)
        d_t = o1 - lam * o2
        rinv = lax.rsqrt(jnp.mean(d_t * d_t, axis=0, keepdims=True) + EPS)
        o = (d_t * rinv * gain_col).T
        rows = slice(i * tq, (i + 1) * tq)
        o_ref[0, rows, :] = (o * _silu(z_ref[0, rows, :].astype(F32))).astype(BF16)


def _diff_attn(p3, lam_vecs, head_g, *, tq=512):
    b, s, _ = p3.shape
    nq = s // tq
    hq, hk, hv, hz = (off // LANES for off in (_OFF_AQ, _OFF_AK, _OFF_AV, _OFF_AZ))
    lam_init = 0.8 - 0.6 * math.exp(-0.3 * 0)
    below = [(qi, kb) for qi in range(nq) for kb in range(qi)]
    assert nq % 2 == 0 and len(below) % 2 == 0
    visit_q = jnp.asarray([qi for qi, _ in below], jnp.int32)
    visit_k = jnp.asarray([kb for _, kb in below], jnp.int32)
    seq = lambda col0: pl.BlockSpec((1, s, LANES), lambda bi, h, vq, vk: (bi, 0, col0 + h))
    return pl.pallas_call(
        functools.partial(_diff_attn_kernel, tq=tq, lam_init=lam_init),
        grid_spec=pltpu.PrefetchScalarGridSpec(
            num_scalar_prefetch=2,
            grid=(b, DA_HEADS),
            in_specs=[
                pl.BlockSpec((4, DA_QK_DIM), lambda bi, h, vq, vk: (0, 0)),
                seq(hq), seq(hk), seq(hv), seq(hz),
                pl.BlockSpec((1, DA_V_DIM), lambda bi, h, vq, vk: (0, 0)),
            ],
            out_specs=pl.BlockSpec((1, s, LANES), lambda bi, h, vq, vk: (bi, 0, h)),
            scratch_shapes=[
                pltpu.VMEM((nq, LANES + SUM_ROWS, tq), BF16),
                pltpu.VMEM((nq, LANES, 2 * tq), BF16),
                pltpu.VMEM((nq, LANES + SUM_ROWS, 2 * tq), F32),
                pltpu.VMEM((nq, 1, 2 * tq), F32),
                pltpu.VMEM((tq, 2 * tq), F32),
                pltpu.VMEM((tq, 2 * tq), F32),
                pltpu.VMEM((1, 2 * tq), F32),
                pltpu.VMEM((1, 2 * tq), F32),
            ],
        ),
        out_shape=jax.ShapeDtypeStruct((b, s, DA_WIDTH), BF16),
        compiler_params=pltpu.CompilerParams(
            dimension_semantics=("arbitrary", "arbitrary"), vmem_limit_bytes=VMEM_LIMIT),
        name="diff_attn",
    )(visit_q, visit_k, lam_vecs, p3, p3, p3, p3, head_g)


def _mlstm_kernel(qp_ref, kp_ref, v_ref, o_ref, z_ref, gt_ref, cwq_ref, cwk_ref, cbq_ref, cbk_ref,
                  hg_ref, y_ref, xq_ref, xk_ref, ic_ref, bc_ref, c_ref):
    h = pl.program_id(1)
    s = qp_ref.shape[1]
    L = ML_CHUNK
    nc = s // L
    pad = 8

    zeros = jnp.zeros((pad, ML_QK_DIM), F32)
    xq_ref[0:pad, :] = zeros
    xk_ref[0:pad, :] = zeros
    xq_ref[pad:, :] = qp_ref[0].astype(F32)
    xk_ref[pad:, :] = kp_ref[0].astype(F32)

    ic = gt_ref[h]
    fp = gt_ref[ML_HEADS + h]
    lf = jnp.minimum(fp, 0.0) - jnp.log(1.0 + jnp.exp(-jnp.abs(fp)))
    r_i = lax.broadcasted_iota(jnp.int32, (L, L), 0)
    c_i = lax.broadcasted_iota(jnp.int32, (L, L), 1)
    tri_u = (r_i <= c_i).astype(BF16)
    lf_hi = lf.astype(BF16)
    lf_lo = (lf - lf_hi.astype(F32)).astype(BF16)
    ic_ref[...] = ic
    bc_ref[...] = _dot(lf_hi, tri_u) + _dot(lf_lo, tri_u)

    c_ref[...] = jnp.zeros_like(c_ref)
    causal = c_i <= r_i
    ones_col = (lax.broadcasted_iota(jnp.int32, (L, LANES), 1) == 0).astype(BF16)
    cwq = cwq_ref[...]
    cwk = cwk_ref[...]

    def conv(x_ref, r0, w, bias):
        xw = x_ref[pl.ds(r0, L + pad), :]
        acc = bias
        for j in range(CONV_WIDTH):
            lo = pad - (CONV_WIDTH - 1) + j
            acc = acc + w[j:j + 1, :] * xw[lo:lo + L, :]
        return _silu(acc)

    def chunk(c, m_prev):
        r0 = pl.multiple_of(c * L, L)
        q = (conv(xq_ref, r0, cwq, cbq_ref[...]) * ML_QK_DIM ** -0.5).astype(BF16)
        k_t = conv(xk_ref, r0, cwk, cbk_ref[...]).T
        v_ext = jnp.concatenate([v_ref[0, pl.ds(r0, L), :], ones_col], axis=1)

        ic_row = ic_ref[pl.ds(c, 1), :]
        bc_row = bc_ref[pl.ds(c, 1), :]
        bc_col = jnp.broadcast_to(bc_row, (8, L)).T[:, 0:1]
        b_last = bc_row[:, L - 1:L]

        log_d = jnp.where(causal, bc_col - bc_row + ic_row, NEG_BIG)
        log_inter = bc_col + m_prev
        m_t = jnp.maximum(log_inter, jnp.max(log_d, axis=-1, keepdims=True))
        dmat = jnp.exp(log_d - m_t)
        inter_w = jnp.exp(log_inter - m_t)
        qk = (_dot(q, k_t.astype(BF16)) * dmat).astype(BF16)
        nd = _dot(qk, v_ext) + inter_w * _dot(q, c_ref[...].astype(BF16))
        num = nd[:, 0:ML_V_DIM]
        den = nd[:, ML_V_DIM:ML_V_DIM + 1]
        hm = num / jnp.maximum(jnp.abs(den), jnp.exp(-m_t))
        hm = hm * _sigmoid(o_ref[0, pl.ds(r0, L), :].astype(F32))
        hm = hm * _rms_scale(hm) * hg_ref[...]
        y_ref[0, pl.ds(r0, L), :] = (hm * _silu(z_ref[0, pl.ds(r0, L), :].astype(F32))).astype(BF16)

        g_row = b_last - bc_row + ic_row
        m_new = jnp.maximum(b_last + m_prev, jnp.max(g_row, axis=-1, keepdims=True))
        decay = jnp.exp(b_last + m_prev - m_new)
        w_row = jnp.exp(g_row - m_new)
        c_ref[...] = decay * c_ref[...] + _dot((k_t * w_row).astype(BF16), v_ext)
        return m_new

    lax.fori_loop(0, nc, chunk, jnp.zeros((1, 1), F32), unroll=8)


def _mlstm(p3, gates_t, conv_w, conv_b, head_g):
    b, s, _ = p3.shape
    nc = s // ML_CHUNK
    cq = _OFF_BQK // ML_QK_DIM
    ck = cq + ML_HEADS
    cv, co, cz = (off // ML_V_DIM for off in (5120, 6144, 7168))
    seq = lambda width, col0: pl.BlockSpec((1, s, width), lambda bi, h: (bi, 0, col0 + h))
    return pl.pallas_call(
        _mlstm_kernel,
        grid=(b, ML_HEADS),
        in_specs=[
            seq(ML_QK_DIM, cq), seq(ML_QK_DIM, ck), seq(ML_V_DIM, cv), seq(ML_V_DIM, co), seq(ML_V_DIM, cz),
            pl.BlockSpec((2 * ML_HEADS, nc, ML_CHUNK), lambda bi, h: (0, bi, 0)),
            pl.BlockSpec((CONV_WIDTH, ML_QK_DIM), lambda bi, h: (0, h)),
            pl.BlockSpec((CONV_WIDTH, ML_QK_DIM), lambda bi, h: (0, ML_HEADS + h)),
            pl.BlockSpec((1, ML_QK_DIM), lambda bi, h: (0, h)),
            pl.BlockSpec((1, ML_QK_DIM), lambda bi, h: (0, ML_HEADS + h)),
            pl.BlockSpec((1, ML_V_DIM), lambda bi, h: (0, 0)),
        ],
        out_specs=pl.BlockSpec((1, s, ML_V_DIM), lambda bi, h: (bi, 0, h)),
        out_shape=jax.ShapeDtypeStruct((b, s, ML_WIDTH), BF16),
        scratch_shapes=[
            pltpu.VMEM((s + 8, ML_QK_DIM), F32),
            pltpu.VMEM((s + 8, ML_QK_DIM), F32),
            pltpu.VMEM((nc, ML_CHUNK), F32),
            pltpu.VMEM((nc, ML_CHUNK), F32),
            pltpu.VMEM((ML_QK_DIM, ML_V_DIM + LANES), F32),
        ],
        compiler_params=pltpu.CompilerParams(
            dimension_semantics=("arbitrary", "arbitrary"), vmem_limit_bytes=VMEM_LIMIT),
        name="mlstm",
    )(p3, p3, p3, p3, p3, gates_t, conv_w, conv_w, conv_b, conv_b, head_g)


def _l0_out_proj_kernel(ya_ref, yb_ref, x_ref, wa_ref, wb_ref, g_ref, h_ref):
    y = _dot(ya_ref[...], wa_ref[...]) + _dot(yb_ref[...], wb_ref[...])
    h_ref[...] = x_ref[...] + y * _rms_scale(y) * g_ref[...]


def _l0_out_proj(ya, yb, x2, w_out, post_g, *, tm=1024):
    m = x2.shape[0]
    assert DA_WIDTH == ML_WIDTH
    row = lambda width: pl.BlockSpec((tm, width), lambda i: (i, 0))
    w_rows = lambda blk: pl.BlockSpec((DA_WIDTH, D_MODEL), lambda i: (blk, 0))
    return pl.pallas_call(
        _l0_out_proj_kernel,
        grid=(m // tm,),
        in_specs=[row(DA_WIDTH), row(ML_WIDTH), row(D_MODEL),
                  w_rows(0), w_rows(1), pl.BlockSpec((1, D_MODEL), lambda i: (0, 0))],
        out_specs=row(D_MODEL),
        out_shape=jax.ShapeDtypeStruct((m, D_MODEL), F32),
        compiler_params=pltpu.CompilerParams(
            dimension_semantics=("arbitrary",), vmem_limit_bytes=VMEM_LIMIT),
        name="l0_out_proj",
    )(ya, yb, x2, w_out, w_out, post_g)


def _l1_gmlp_kernel(h_ref, pre_g_ref, wu_ref, wv_ref, wz_ref, sg_g_ref, wsp_ref, bsp_ref,
                    wo_ref, post_g_ref, o_ref, vn_ref, y_ref, *, tm):
    hres = h_ref[...]
    hn = (hres * _rms_scale(hres) * pre_g_ref[...]).astype(BF16)

    v = _gelu_tanh(_dot(hn, wv_ref[...]))
    vn_ref[...] = (v * _rms_scale(v) * sg_g_ref[...]).astype(BF16)

    r_i = lax.broadcasted_iota(jnp.int32, (SG_CHUNK, SG_CHUNK), 0)
    c_i = lax.broadcasted_iota(jnp.int32, (SG_CHUNK, SG_CHUNK), 1)
    causal = c_i <= r_i
    for g in range(SG_GROUPS):
        cols = slice(g * SG_GROUP_DIM, (g + 1) * SG_GROUP_DIM)
        u = _gelu_tanh(_dot(hn, wu_ref[:, cols]))
        gate = _silu(_dot(hn, wz_ref[:, cols]))
        wm = jnp.where(causal, wsp_ref[g], 0.0).astype(BF16)
        bias = bsp_ref[g]
        for c in range(tm // SG_CHUNK):
            rows = slice(c * SG_CHUNK, (c + 1) * SG_CHUNK)
            vs = _dot(wm, vn_ref[rows, cols]) + bias
            y_ref[rows, cols] = (u[rows] * vs * gate[rows]).astype(BF16)

    out = _dot(y_ref[...], wo_ref[...])
    o_ref[...] = hres + out * _rms_scale(out) * post_g_ref[...]


def _l1_gmlp(h1, pre_g, w_in, sg_g, w_spatial, b_spatial, wo, post_g, *, tm=512):
    m = h1.shape[0]
    row = pl.BlockSpec((tm, D_MODEL), lambda i: (i, 0))
    const2 = lambda shape: pl.BlockSpec(shape, lambda i: (0, 0), pipeline_mode=pl.Buffered(1))
    const3 = lambda shape: pl.BlockSpec(shape, lambda i: (0, 0, 0), pipeline_mode=pl.Buffered(1))
    w_cols = lambda blk: pl.BlockSpec((D_MODEL, SG_WIDTH), lambda i: (0, blk), pipeline_mode=pl.Buffered(1))
    return pl.pallas_call(
        functools.partial(_l1_gmlp_kernel, tm=tm),
        grid=(m // tm,),
        in_specs=[
            row, const2((1, D_MODEL)),
            w_cols(0), w_cols(1), w_cols(2),
            const2((1, SG_WIDTH)),
            const3((SG_GROUPS, SG_CHUNK, SG_CHUNK)), const3((SG_GROUPS, SG_CHUNK, 1)),
            const2((SG_WIDTH, D_MODEL)), const2((1, D_MODEL)),
        ],
        out_specs=row,
        out_shape=jax.ShapeDtypeStruct((m, D_MODEL), F32),
        scratch_shapes=[pltpu.VMEM((tm, SG_WIDTH), BF16), pltpu.VMEM((tm, SG_WIDTH), BF16)],
        compiler_params=pltpu.CompilerParams(
            dimension_semantics=("arbitrary",), vmem_limit_bytes=VMEM_LIMIT),
        name="l1_gmlp",
    )(h1, pre_g, w_in, w_in, w_in, sg_g, w_spatial, b_spatial, wo, post_g)


def kernel(x, l0_pre_g, l0_w_in, l0_b_igate, l0_b_fgate, l0_conv_w, l0_conv_b, l0_lambda_q1, l0_lambda_k1, l0_lambda_q2, l0_lambda_k2, l0_da_head_g, l0_ml_head_g, l0_w_out, l0_post_g, l1_pre_g, l1_w_in, l1_sg_norm_g, l1_w_spatial, l1_b_spatial, l1_w_out, l1_post_g):
    b, s, d = x.shape
    m = b * s
    x2 = x.reshape(m, d)
    row = lambda v: v.reshape(1, -1).astype(F32)

    w_in0 = l0_w_in.astype(BF16)
    w_main = jnp.concatenate([w_in0[:, :_OFF_BI], w_in0[:, _OFF_BO:]], axis=1)
    w_gate = jnp.pad(w_in0[:, _OFF_BI:_OFF_BO], ((0, 0), (0, LANES - 2 * ML_HEADS)))
    b_gate = jnp.pad(jnp.concatenate([l0_b_igate, l0_b_fgate]), (0, LANES - 2 * ML_HEADS)).reshape(1, LANES)
    lam_vecs = jnp.stack([l0_lambda_q1, l0_lambda_k1, l0_lambda_q2, l0_lambda_k2]).astype(F32)

    p, gates_t = _l0_in_proj(x2, row(l0_pre_g), w_main, w_gate, b_gate.astype(F32))
    p3 = p.reshape(b, s, L0_MAIN)
    gates_t = gates_t.reshape(2 * ML_HEADS, m // ML_CHUNK, ML_CHUNK)

    y_a = _diff_attn(p3, lam_vecs, row(l0_da_head_g))
    y_b = _mlstm(p3, gates_t, l0_conv_w.astype(F32), row(l0_conv_b), row(l0_ml_head_g))

    h1 = _l0_out_proj(y_a.reshape(m, DA_WIDTH), y_b.reshape(m, ML_WIDTH), x2,
                      l0_w_out.astype(BF16), row(l0_post_g))

    h2 = _l1_gmlp(h1, row(l1_pre_g), l1_w_in.astype(BF16),
                  row(l1_sg_norm_g), l1_w_spatial.astype(F32),
                  l1_b_spatial.astype(F32).reshape(SG_GROUPS, SG_CHUNK, 1),
                  l1_w_out.astype(BF16), row(l1_post_g))
    return h2.reshape(b, s, d)
```

```python
import functools
import math

import jax
import jax.numpy as jnp
from jax import lax
from jax.experimental import pallas as pl
from jax.experimental.pallas import tpu as pltpu

F32 = jnp.float32
BF16 = jnp.bfloat16

D_MODEL = 1024
EPS = 1e-6
DA_HEADS = 8
DA_QK_DIM = 64
DA_V_DIM = 128
DA_WIDTH = DA_HEADS * DA_V_DIM
ML_HEADS = 4
ML_QK_DIM = 128
ML_V_DIM = 256
ML_WIDTH = ML_HEADS * ML_V_DIM
ML_CHUNK = 128
CONV_WIDTH = 4
SG_GROUPS = 8
SG_CHUNK = 128
SG_WIDTH = 2 * D_MODEL
SG_GROUP_DIM = SG_WIDTH // SG_GROUPS

LANES = 128
SUM_ROWS = 16
VMEM_LIMIT = 56 * 1024 * 1024
NEG_BIG = -1e30
LOG2E = 1.4426950408889634

_OFF_AQ, _OFF_AK, _OFF_AV, _OFF_AZ = 0, 1024, 2048, 3072
_OFF_BQK, _OFF_BV, _OFF_BI, _OFF_BF, _OFF_BO, _OFF_BZ = 4096, 5120, 6144, 6148, 6152, 7176
L0_MAIN = 8192


def _sigmoid(x):
    return 1.0 / (1.0 + jnp.exp(-x))


def _silu(x):
    return x * _sigmoid(x)


def _gelu_tanh(x):
    c = math.sqrt(2.0 / math.pi)
    return 0.5 * x * (1.0 + jnp.tanh(c * (x + 0.044715 * (x * x * x))))


def _rms_scale(x):
    return lax.rsqrt(jnp.mean(x * x, axis=-1, keepdims=True) + EPS)


def _dot(a, b):
    return jnp.dot(a, b, preferred_element_type=F32)


def _l0_in_proj_kernel(x_ref, g_ref, w_ref, cs_ref, wg_ref, bg_ref, p_ref, gt_ref, hn_ref):
    j = pl.program_id(1)

    @pl.when(j == 0)
    def _():
        x = x_ref[...]
        hn = (x * _rms_scale(x) * g_ref[...]).astype(BF16)
        hn_ref[...] = hn
        gates = _dot(hn, wg_ref[...]) + bg_ref[...]
        gt_ref[...] = gates.T[0:8, :]

    p_ref[...] = (_dot(hn_ref[...], w_ref[...]) * cs_ref[...]).astype(BF16)


def _l0_in_proj(x2, pre_g, w_main, w_gate, b_gate, *, tm=2048, tn=2048):
    m = x2.shape[0]
    q_scale = DA_QK_DIM ** -0.5 * LOG2E
    col_scale = jnp.where(jnp.arange(L0_MAIN) < DA_HEADS * 2 * DA_QK_DIM, q_scale, 1.0).astype(F32).reshape(1, -1)
    return pl.pallas_call(
        _l0_in_proj_kernel,
        grid=(m // tm, L0_MAIN // tn),
        in_specs=[
            pl.BlockSpec((tm, D_MODEL), lambda i, j: (i, 0)),
            pl.BlockSpec((1, D_MODEL), lambda i, j: (0, 0)),
            pl.BlockSpec((D_MODEL, tn), lambda i, j: (0, j)),
            pl.BlockSpec((1, tn), lambda i, j: (0, j)),
            pl.BlockSpec((D_MODEL, LANES), lambda i, j: (0, 0)),
            pl.BlockSpec((1, LANES), lambda i, j: (0, 0)),
        ],
        out_specs=[
            pl.BlockSpec((tm, tn), lambda i, j: (i, j)),
            pl.BlockSpec((8, tm), lambda i, j: (0, i)),
        ],
        out_shape=[
            jax.ShapeDtypeStruct((m, L0_MAIN), BF16),
            jax.ShapeDtypeStruct((8, m), F32),
        ],
        scratch_shapes=[pltpu.VMEM((tm, D_MODEL), BF16)],
        compiler_params=pltpu.CompilerParams(
            dimension_semantics=("arbitrary", "arbitrary"), vmem_limit_bytes=VMEM_LIMIT),
        name="l0_in_proj",
    )(x2, pre_g, w_main, col_scale, w_gate, b_gate)


def _diff_attn_kernel(vq_ref, vk_ref, lam_ref, q_ref, k_ref, v_ref, z_ref, g_ref, o_ref,
                      vt_ref, qbd_ref, acc_ref, m_ref, s0_ref, s1_ref, cmax0_ref, cmax1_ref, *, tq, lam_init):
    nq = vt_ref.shape[0]
    tk = tq
    n_off = vq_ref.shape[0]

    lam_v = lam_ref[...]
    lam = (jnp.exp(jnp.sum(lam_v[0:1] * lam_v[1:2], axis=-1, keepdims=True))
           - jnp.exp(jnp.sum(lam_v[2:3] * lam_v[3:4], axis=-1, keepdims=True)) + lam_init)

    half = tq // 2
    ones_row = (lax.broadcasted_iota(jnp.int32, (SUM_ROWS, tk), 0) == 0).astype(BF16)
    first = lax.broadcasted_iota(jnp.int32, (LANES, half), 0) < DA_QK_DIM
    zero = jnp.zeros((LANES, half), BF16)
    for i in range(nq):
        rows = slice(i * tq, (i + 1) * tq)
        vt_ref[i, 0:LANES, :] = v_ref[0, rows, :].T
        vt_ref[i, LANES:LANES + SUM_ROWS, :] = ones_row
        qt = q_ref[0, rows, :].T
        parts = []
        for qh in (qt[:, 0:half], qt[:, half:tq]):
            parts += [jnp.where(first, qh, zero), jnp.where(first, zero, qh)]
        qbd_ref[i] = jnp.concatenate(parts, axis=1)

    buf0, buf1 = (s0_ref, cmax0_ref), (s1_ref, cmax1_ref)

    def pipelined(n, issue, consume, issue_next_phase):
        def pair(i, issue_after):
            issue(buf1, i + 1)
            consume(buf0, i)
            issue_after()
            consume(buf1, i + 1)

        def body(t, carry):
            pair(2 * t, lambda: issue(buf0, 2 * t + 2))
            return carry

        lax.fori_loop(0, n // 2 - 1, body, 0, unroll=2)
        pair(n - 2, issue_next_phase)

    tri = (lax.broadcasted_iota(jnp.int32, (half, tq), 0)
           <= (lax.broadcasted_iota(jnp.int32, (half, tq), 1) & (half - 1)))

    def diag_scores(buf, i):
        s_ref, cmax_ref = buf
        r0 = pl.multiple_of(i * tk, tk)
        s_a = _dot(k_ref[0, pl.ds(r0, half), :], qbd_ref[i])
        s_b = _dot(k_ref[0, pl.ds(r0 + half, half), :], qbd_ref[i, :, tq:2 * tq])
        left = jnp.where(tri, s_a[:, 0:tq], NEG_BIG)
        right = s_a[:, tq:2 * tq]
        s_b = jnp.where(tri, s_b, NEG_BIG)
        s_ref[0:half, 0:tq] = left
        s_ref[0:half, tq:2 * tq] = right
        s_ref[half:tq, tq:2 * tq] = s_b
        cmax_ref[:, 0:tq] = jnp.max(left, axis=0, keepdims=True)
        cmax_ref[:, tq:2 * tq] = jnp.maximum(jnp.max(right, axis=0, keepdims=True),
                                             jnp.max(s_b, axis=0, keepdims=True))

    def diag_softmax_pv(buf, i):
        s_ref, cmax_ref = buf
        m_new = cmax_ref[...]
        p_a = jnp.exp2(s_ref[0:half, :] - m_new).astype(BF16)
        p_b = jnp.exp2(s_ref[half:tq, tq:2 * tq] - m_new[:, tq:2 * tq]).astype(BF16)
        pv_a = _dot(vt_ref[i, :, 0:half], p_a)
        pv_b = _dot(vt_ref[i, :, half:tq], p_b)
        acc_ref[i, :, 0:tq] = pv_a[:, 0:tq]
        acc_ref[i, :, tq:2 * tq] = pv_a[:, tq:2 * tq] + pv_b
        m_ref[i] = m_new

    def scores(buf, i):
        s_ref, cmax_ref = buf
        k = k_ref[0, pl.ds(pl.multiple_of(vk_ref[i] * tk, tk), tk), :]
        s = _dot(k, qbd_ref[vq_ref[i]])
        s_ref[...] = s
        cmax_ref[...] = jnp.max(s, axis=0, keepdims=True)

    def softmax_pv(buf, i):
        s_ref, cmax_ref = buf
        qi = vq_ref[i]
        m = m_ref[qi]
        m_new = jnp.maximum(m, cmax_ref[...])
        alpha = jnp.exp2(m - m_new)
        p = jnp.exp2(s_ref[...] - m_new).astype(BF16)
        acc_ref[qi] = alpha * acc_ref[qi] + _dot(vt_ref[vk_ref[i]], p)
        m_ref[qi] = m_new

    diag_scores(buf0, 0)
    pipelined(nq, diag_scores, diag_softmax_pv, lambda: scores(buf0, 0))
    pipelined(n_off, scores, softmax_pv, lambda: None)

    gain_col = jnp.broadcast_to(g_ref[...] * (1.0 - lam_init), (8, LANES)).T[:, 0:1]
    for i in range(nq):
        acc = acc_ref[i]
        o_t = acc[0:LANES, :] / acc[LANES:LANES + 1, :]
        o1 = jnp.concatenate([o_t[:, 0:half], o_t[:, tq:tq + half]], axis=1)
        o2 = jnp.concatenate([o_t[:, half:tq], o_t[:, tq + half:2 * tq]], axis=1)
        d_t = o1 - lam * o2
        rinv = lax.rsqrt(jnp.mean(d_t * d_t, axis=0, keepdims=True) + EPS)
        o = (d_t * rinv * gain_col).T
        rows = slice(i * tq, (i + 1) * tq)
        o_ref[0, rows, :] = (o * _silu(z_ref[0, rows, :].astype(F32))).astype(BF16)


def _diff_attn(p3, lam_vecs, head_g, *, tq=512):
    b, s, _ = p3.shape
    nq = s // tq
    hq, hk, hv, hz = (off // LANES for off in (_OFF_AQ, _OFF_AK, _OFF_AV, _OFF_AZ))
    lam_init = 0.8 - 0.6 * math.exp(-0.3 * 0)
    below = [(qi, kb) for qi in range(nq) for kb in range(qi)]
    assert nq % 2 == 0 and len(below) % 2 == 0
    visit_q = jnp.asarray([qi for qi, _ in below], jnp.int32)
    visit_k = jnp.asarray([kb for _, kb in below], jnp.int32)
    seq = lambda col0: pl.BlockSpec((1, s, LANES), lambda bi, h, vq, vk: (bi, 0, col0 + h))
    return pl.pallas_call(
        functools.partial(_diff_attn_kernel, tq=tq, lam_init=lam_init),
        grid_spec=pltpu.PrefetchScalarGridSpec(
            num_scalar_prefetch=2,
            grid=(b, DA_HEADS),
            in_specs=[
                pl.BlockSpec((4, DA_QK_DIM), lambda bi, h, vq, vk: (0, 0)),
                seq(hq), seq(hk), seq(hv), seq(hz),
                pl.BlockSpec((1, DA_V_DIM), lambda bi, h, vq, vk: (0, 0)),
            ],
            out_specs=pl.BlockSpec((1, s, LANES), lambda bi, h, vq, vk: (bi, 0, h)),
            scratch_shapes=[
                pltpu.VMEM((nq, LANES + SUM_ROWS, tq), BF16),
                pltpu.VMEM((nq, LANES, 2 * tq), BF16),
                pltpu.VMEM((nq, LANES + SUM_ROWS, 2 * tq), F32),
                pltpu.VMEM((nq, 1, 2 * tq), F32),
                pltpu.VMEM((tq, 2 * tq), F32),
                pltpu.VMEM((tq, 2 * tq), F32),
                pltpu.VMEM((1, 2 * tq), F32),
                pltpu.VMEM((1, 2 * tq), F32),
            ],
        ),
        out_shape=jax.ShapeDtypeStruct((b, s, DA_WIDTH), BF16),
        compiler_params=pltpu.CompilerParams(
            dimension_semantics=("arbitrary", "arbitrary"), vmem_limit_bytes=VMEM_LIMIT),
        name="diff_attn",
    )(visit_q, visit_k, lam_vecs, p3, p3, p3, p3, head_g)


def _mlstm_kernel(qp_ref, kp_ref, v_ref, o_ref, z_ref, gt_ref, cwq_ref, cwk_ref, cbq_ref, cbk_ref,
                  hg_ref, y_ref, xq_ref, xk_ref, ic_ref, bc_ref, c_ref):
    h = pl.program_id(1)
    s = qp_ref.shape[1]
    L = ML_CHUNK
    nc = s // L
    pad = 8

    zeros = jnp.zeros((pad, ML_QK_DIM), F32)
    xq_ref[0:pad, :] = zeros
    xk_ref[0:pad, :] = zeros
    xq_ref[pad:, :] = qp_ref[0].astype(F32)
    xk_ref[pad:, :] = kp_ref[0].astype(F32)

    ic = gt_ref[h]
    fp = gt_ref[ML_HEADS + h]
    lf = jnp.minimum(fp, 0.0) - jnp.log(1.0 + jnp.exp(-jnp.abs(fp)))
    r_i = lax.broadcasted_iota(jnp.int32, (L, L), 0)
    c_i = lax.broadcasted_iota(jnp.int32, (L, L), 1)
    tri_u = (r_i <= c_i).astype(BF16)
    lf_hi = lf.astype(BF16)
    lf_lo = (lf - lf_hi.astype(F32)).astype(BF16)
    ic_ref[...] = ic
    bc_ref[...] = _dot(lf_hi, tri_u) + _dot(lf_lo, tri_u)

    c_ref[...] = jnp.zeros_like(c_ref)
    causal = c_i <= r_i
    ones_col = (lax.broadcasted_iota(jnp.int32, (L, LANES), 1) == 0).astype(BF16)
    cwq = cwq_ref[...]
    cwk = cwk_ref[...]

    def conv(x_ref, r0, w, bias):
        xw = x_ref[pl.ds(r0, L + pad), :]
        acc = bias
        for j in range(CONV_WIDTH):
            lo = pad - (CONV_WIDTH - 1) + j
            acc = acc + w[j:j + 1, :] * xw[lo:lo + L, :]
        return _silu(acc)

    def chunk(c, m_prev):
        r0 = pl.multiple_of(c * L, L)
        q = (conv(xq_ref, r0, cwq, cbq_ref[...]) * ML_QK_DIM ** -0.5).astype(BF16)
        k_t = conv(xk_ref, r0, cwk, cbk_ref[...]).T
        v_ext = jnp.concatenate([v_ref[0, pl.ds(r0, L), :], ones_col], axis=1)

        ic_row = ic_ref[pl.ds(c, 1), :]
        bc_row = bc_ref[pl.ds(c, 1), :]
        bc_col = jnp.broadcast_to(bc_row, (8, L)).T[:, 0:1]
        b_last = bc_row[:, L - 1:L]

        log_d = jnp.where(causal, bc_col - bc_row + ic_row, NEG_BIG)
        log_inter = bc_col + m_prev
        m_t = jnp.maximum(log_inter, jnp.max(log_d, axis=-1, keepdims=True))
        dmat = jnp.exp(log_d - m_t)
        inter_w = jnp.exp(log_inter - m_t)
        qk = (_dot(q, k_t.astype(BF16)) * dmat).astype(BF16)
        nd = _dot(qk, v_ext) + inter_w * _dot(q, c_ref[...].astype(BF16))
        num = nd[:, 0:ML_V_DIM]
        den = nd[:, ML_V_DIM:ML_V_DIM + 1]
        hm = num / jnp.maximum(jnp.abs(den), jnp.exp(-m_t))
        hm = hm * _sigmoid(o_ref[0, pl.ds(r0, L), :].astype(F32))
        hm = hm * _rms_scale(hm) * hg_ref[...]
        y_ref[0, pl.ds(r0, L), :] = (hm * _silu(z_ref[0, pl.ds(r0, L), :].astype(F32))).astype(BF16)

        g_row = b_last - bc_row + ic_row
        m_new = jnp.maximum(b_last + m_prev, jnp.max(g_row, axis=-1, keepdims=True))
        decay = jnp.exp(b_last + m_prev - m_new)
        w_row = jnp.exp(g_row - m_new)
        c_ref[...] = decay * c_ref[...] + _dot((k_t * w_row).astype(BF16), v_ext)
        return m_new

    lax.fori_loop(0, nc, chunk, jnp.zeros((1, 1), F32), unroll=8)


def _mlstm(p3, gates_t, conv_w, conv_b, head_g):
    b, s, _ = p3.shape
    nc = s // ML_CHUNK
    cq = _OFF_BQK // ML_QK_DIM
    ck = cq + ML_HEADS
    cv, co, cz = (off // ML_V_DIM for off in (5120, 6144, 7168))
    seq = lambda width, col0: pl.BlockSpec((1, s, width), lambda bi, h: (bi, 0, col0 + h))
    return pl.pallas_call(
        _mlstm_kernel,
        grid=(b, ML_HEADS),
        in_specs=[
            seq(ML_QK_DIM, cq), seq(ML_QK_DIM, ck), seq(ML_V_DIM, cv), seq(ML_V_DIM, co), seq(ML_V_DIM, cz),
            pl.BlockSpec((2 * ML_HEADS, nc, ML_CHUNK), lambda bi, h: (0, bi, 0)),
            pl.BlockSpec((CONV_WIDTH, ML_QK_DIM), lambda bi, h: (0, h)),
            pl.BlockSpec((CONV_WIDTH, ML_QK_DIM), lambda bi, h: (0, ML_HEADS + h)),
            pl.BlockSpec((1, ML_QK_DIM), lambda bi, h: (0, h)),
            pl.BlockSpec((1, ML_QK_DIM), lambda bi, h: (0, ML_HEADS + h)),
            pl.BlockSpec((1, ML_V_DIM), lambda bi, h: (0, 0)),
        ],
        out_specs=pl.BlockSpec((1, s, ML_V_DIM), lambda bi, h: (bi, 0, h)),
        out_shape=jax.ShapeDtypeStruct((b, s, ML_WIDTH), BF16),
        scratch_shapes=[
            pltpu.VMEM((s + 8, ML_QK_DIM), F32),
            pltpu.VMEM((s + 8, ML_QK_DIM), F32),
            pltpu.VMEM((nc, ML_CHUNK), F32),
            pltpu.VMEM((nc, ML_CHUNK), F32),
            pltpu.VMEM((ML_QK_DIM, ML_V_DIM + LANES), F32),
        ],
        compiler_params=pltpu.CompilerParams(
            dimension_semantics=("arbitrary", "arbitrary"), vmem_limit_bytes=VMEM_LIMIT),
        name="mlstm",
    )(p3, p3, p3, p3, p3, gates_t, conv_w, conv_w, conv_b, conv_b, head_g)


def _l0_out_proj_kernel(ya_ref, yb_ref, x_ref, wa_ref, wb_ref, g_ref, h_ref):
    y = _dot(ya_ref[...], wa_ref[...]) + _dot(yb_ref[...], wb_ref[...])
    h_ref[...] = x_ref[...] + y * _rms_scale(y) * g_ref[...]


def _l0_out_proj(ya, yb, x2, w_out, post_g, *, tm=1024):
    m = x2.shape[0]
    assert DA_WIDTH == ML_WIDTH
    row = lambda width: pl.BlockSpec((tm, width), lambda i: (i, 0))
    w_rows = lambda blk: pl.BlockSpec((DA_WIDTH, D_MODEL), lambda i: (blk, 0))
    return pl.pallas_call(
        _l0_out_proj_kernel,
        grid=(m // tm,),
        in_specs=[row(DA_WIDTH), row(ML_WIDTH), row(D_MODEL),
                  w_rows(0), w_rows(1), pl.BlockSpec((1, D_MODEL), lambda i: (0, 0))],
        out_specs=row(D_MODEL),
        out_shape=jax.ShapeDtypeStruct((m, D_MODEL), F32),
        compiler_params=pltpu.CompilerParams(
            dimension_semantics=("arbitrary",), vmem_limit_bytes=VMEM_LIMIT),
        name="l0_out_proj",
    )(ya, yb, x2, w_out, w_out, post_g)


def _l1_gmlp_kernel(h_ref, pre_g_ref, wu_ref, wv_ref, wz_ref, sg_g_ref, wsp_ref, bsp_ref,
                    wo_ref, post_g_ref, o_ref, vn_ref, y_ref, *, tm):
    hres = h_ref[...]
    hn = (hres * _rms_scale(hres) * pre_g_ref[...]).astype(BF16)

    v = _gelu_tanh(_dot(hn, wv_ref[...]))
    vn_ref[...] = (v * _rms_scale(v) * sg_g_ref[...]).astype(BF16)

    r_i = lax.broadcasted_iota(jnp.int32, (SG_CHUNK, SG_CHUNK), 0)
    c_i = lax.broadcasted_iota(jnp.int32, (SG_CHUNK, SG_CHUNK), 1)
    causal = c_i <= r_i
    for g in range(SG_GROUPS):
        cols = slice(g * SG_GROUP_DIM, (g + 1) * SG_GROUP_DIM)
        u = _gelu_tanh(_dot(hn, wu_ref[:, cols]))
        gate = _silu(_dot(hn, wz_ref[:, cols]))
        wm = jnp.where(causal, wsp_ref[g], 0.0).astype(BF16)
        bias = bsp_ref[g]
        for c in range(tm // SG_CHUNK):
            rows = slice(c * SG_CHUNK, (c + 1) * SG_CHUNK)
            vs = _dot(wm, vn_ref[rows, cols]) + bias
            y_ref[rows, cols] = (u[rows] * vs * gate[rows]).astype(BF16)

    out = _dot(y_ref[...], wo_ref[...])
    o_ref[...] = hres + out * _rms_scale(out) * post_g_ref[...]


def _l1_gmlp(h1, pre_g, w_in, sg_g, w_spatial, b_spatial, wo, post_g, *, tm=1024):
    m = h1.shape[0]
    row = pl.BlockSpec((tm, D_MODEL), lambda i: (i, 0))
    const2 = lambda shape: pl.BlockSpec(shape, lambda i: (0, 0), pipeline_mode=pl.Buffered(1))
    const3 = lambda shape: pl.BlockSpec(shape, lambda i: (0, 0, 0), pipeline_mode=pl.Buffered(1))
    w_cols = lambda blk: pl.BlockSpec((D_MODEL, SG_WIDTH), lambda i: (0, blk), pipeline_mode=pl.Buffered(1))
    return pl.pallas_call(
        functools.partial(_l1_gmlp_kernel, tm=tm),
        grid=(m // tm,),
        in_specs=[
            row, const2((1, D_MODEL)),
            w_cols(0), w_cols(1), w_cols(2),
            const2((1, SG_WIDTH)),
            const3((SG_GROUPS, SG_CHUNK, SG_CHUNK)), const3((SG_GROUPS, SG_CHUNK, 1)),
            const2((SG_WIDTH, D_MODEL)), const2((1, D_MODEL)),
        ],
        out_specs=row,
        out_shape=jax.ShapeDtypeStruct((m, D_MODEL), F32),
        scratch_shapes=[pltpu.VMEM((tm, SG_WIDTH), BF16), pltpu.VMEM((tm, SG_WIDTH), BF16)],
        compiler_params=pltpu.CompilerParams(
            dimension_semantics=("arbitrary",), vmem_limit_bytes=VMEM_LIMIT),
        name="l1_gmlp",
    )(h1, pre_g, w_in, w_in, w_in, sg_g, w_spatial, b_spatial, wo, post_g)


def kernel(x, l0_pre_g, l0_w_in, l0_b_igate, l0_b_fgate, l0_conv_w, l0_conv_b, l0_lambda_q1, l0_lambda_k1, l0_lambda_q2, l0_lambda_k2, l0_da_head_g, l0_ml_head_g, l0_w_out, l0_post_g, l1_pre_g, l1_w_in, l1_sg_norm_g, l1_w_spatial, l1_b_spatial, l1_w_out, l1_post_g):
    b, s, d = x.shape
    m = b * s
    x2 = x.reshape(m, d)
    row = lambda v: v.reshape(1, -1).astype(F32)

    w_in0 = l0_w_in.astype(BF16)
    w_main = jnp.concatenate([w_in0[:, :_OFF_BI], w_in0[:, _OFF_BO:]], axis=1)
    w_gate = jnp.pad(w_in0[:, _OFF_BI:_OFF_BO], ((0, 0), (0, LANES - 2 * ML_HEADS)))
    b_gate = jnp.pad(jnp.concatenate([l0_b_igate, l0_b_fgate]), (0, LANES - 2 * ML_HEADS)).reshape(1, LANES)
    lam_vecs = jnp.stack([l0_lambda_q1, l0_lambda_k1, l0_lambda_q2, l0_lambda_k2]).astype(F32)

    p, gates_t = _l0_in_proj(x2, row(l0_pre_g), w_main, w_gate, b_gate.astype(F32))
    p3 = p.reshape(b, s, L0_MAIN)
    gates_t = gates_t.reshape(2 * ML_HEADS, m // ML_CHUNK, ML_CHUNK)

    y_a = _diff_attn(p3, lam_vecs, row(l0_da_head_g))
    y_b = _mlstm(p3, gates_t, l0_conv_w.astype(F32), row(l0_conv_b), row(l0_ml_head_g))

    h1 = _l0_out_proj(y_a.reshape(m, DA_WIDTH), y_b.reshape(m, ML_WIDTH), x2,
                      l0_w_out.astype(BF16), row(l0_post_g))

    h2 = _l1_gmlp(h1, row(l1_pre_g), l1_w_in.astype(BF16),
                  row(l1_sg_norm_g), l1_w_spatial.astype(F32),
                  l1_b_spatial.astype(F32).reshape(SG_GROUPS, SG_CHUNK, 1),
                  l1_w_out.astype(BF16), row(l1_post_g))
    return h2.reshape(b, s, d)
```

```python
import functools
import math

import jax
import jax.numpy as jnp
from jax import lax
from jax.experimental import pallas as pl
from jax.experimental.pallas import tpu as pltpu

F32 = jnp.float32
BF16 = jnp.bfloat16

D_MODEL = 1024
EPS = 1e-6
DA_HEADS = 8
DA_QK_DIM = 64
DA_V_DIM = 128
DA_WIDTH = DA_HEADS * DA_V_DIM
ML_HEADS = 4
ML_QK_DIM = 128
ML_V_DIM = 256
ML_WIDTH = ML_HEADS * ML_V_DIM
ML_CHUNK = 128
CONV_WIDTH = 4
SG_GROUPS = 8
SG_CHUNK = 128
SG_WIDTH = 2 * D_MODEL
SG_GROUP_DIM = SG_WIDTH // SG_GROUPS

LANES = 128
SUM_ROWS = 16
VMEM_LIMIT = 56 * 1024 * 1024
NEG_BIG = -1e30
LOG2E = 1.4426950408889634

_OFF_AQ, _OFF_AK, _OFF_AV, _OFF_AZ = 0, 1024, 2048, 3072
_OFF_BQK, _OFF_BV, _OFF_BI, _OFF_BF, _OFF_BO, _OFF_BZ = 4096, 5120, 6144, 6148, 6152, 7176
L0_MAIN = 8192


def _sigmoid(x):
    return 1.0 / (1.0 + jnp.exp2(x * -LOG2E))


def _silu(x):
    return x * _sigmoid(x)


def _gelu_tanh(x):
    c = math.sqrt(2.0 / math.pi)
    return 0.5 * x * (1.0 + jnp.tanh(c * (x + 0.044715 * (x * x * x))))


def _rms_scale(x):
    return lax.rsqrt(jnp.mean(x * x, axis=-1, keepdims=True) + EPS)


def _dot(a, b):
    return jnp.dot(a, b, preferred_element_type=F32)


def _l0_in_proj_kernel(x_ref, g_ref, w_ref, cs_ref, wg_ref, bg_ref, p_ref, gt_ref, hn_ref):
    j = pl.program_id(1)

    @pl.when(j == 0)
    def _():
        x = x_ref[...]
        hn = (x * _rms_scale(x) * g_ref[...]).astype(BF16)
        hn_ref[...] = hn
        gates = _dot(hn, wg_ref[...]) + bg_ref[...]
        gt_ref[...] = gates.T[0:8, :]

    p_ref[...] = (_dot(hn_ref[...], w_ref[...]) * cs_ref[...]).astype(BF16)


def _l0_in_proj(x2, pre_g, w_main, w_gate, b_gate, *, tm=2048, tn=2048):
    m = x2.shape[0]
    q_scale = DA_QK_DIM ** -0.5 * LOG2E
    col_scale = jnp.where(jnp.arange(L0_MAIN) < DA_HEADS * 2 * DA_QK_DIM, q_scale, 1.0).astype(F32).reshape(1, -1)
    return pl.pallas_call(
        _l0_in_proj_kernel,
        grid=(m // tm, L0_MAIN // tn),
        in_specs=[
            pl.BlockSpec((tm, D_MODEL), lambda i, j: (i, 0)),
            pl.BlockSpec((1, D_MODEL), lambda i, j: (0, 0)),
            pl.BlockSpec((D_MODEL, tn), lambda i, j: (0, j)),
            pl.BlockSpec((1, tn), lambda i, j: (0, j)),
            pl.BlockSpec((D_MODEL, LANES), lambda i, j: (0, 0)),
            pl.BlockSpec((1, LANES), lambda i, j: (0, 0)),
        ],
        out_specs=[
            pl.BlockSpec((tm, tn), lambda i, j: (i, j)),
            pl.BlockSpec((8, tm), lambda i, j: (0, i)),
        ],
        out_shape=[
            jax.ShapeDtypeStruct((m, L0_MAIN), BF16),
            jax.ShapeDtypeStruct((8, m), F32),
        ],
        scratch_shapes=[pltpu.VMEM((tm, D_MODEL), BF16)],
        compiler_params=pltpu.CompilerParams(
            dimension_semantics=("arbitrary", "arbitrary"), vmem_limit_bytes=VMEM_LIMIT),
        name="l0_in_proj",
    )(x2, pre_g, w_main, col_scale, w_gate, b_gate)


def _diff_attn_kernel(vq_ref, vk_ref, lam_ref, q_ref, k_ref, v_ref, z_ref, g_ref, o_ref,
                      vt_ref, qbd_ref, acc_ref, m_ref, s0_ref, s1_ref, cmax0_ref, cmax1_ref, *, tq, lam_init):
    nq = vt_ref.shape[0]
    tk = tq
    n_off = vq_ref.shape[0]

    lam_v = lam_ref[...]
    lam = (jnp.exp(jnp.sum(lam_v[0:1] * lam_v[1:2], axis=-1, keepdims=True))
           - jnp.exp(jnp.sum(lam_v[2:3] * lam_v[3:4], axis=-1, keepdims=True)) + lam_init)

    half = tq // 2
    ones_row = (lax.broadcasted_iota(jnp.int32, (SUM_ROWS, tk), 0) == 0).astype(BF16)
    first = lax.broadcasted_iota(jnp.int32, (LANES, half), 0) < DA_QK_DIM
    zero = jnp.zeros((LANES, half), BF16)
    for i in range(nq):
        rows = slice(i * tq, (i + 1) * tq)
        vt_ref[i, 0:LANES, :] = v_ref[0, rows, :].T
        vt_ref[i, LANES:LANES + SUM_ROWS, :] = ones_row
        qt = q_ref[0, rows, :].T
        parts = []
        for qh in (qt[:, 0:half], qt[:, half:tq]):
            parts += [jnp.where(first, qh, zero), jnp.where(first, zero, qh)]
        qbd_ref[i] = jnp.concatenate(parts, axis=1)

    buf0, buf1 = (s0_ref, cmax0_ref), (s1_ref, cmax1_ref)

    def pipelined(n, issue, consume, issue_next_phase):
        def pair(i, issue_after):
            issue(buf1, i + 1)
            consume(buf0, i)
            issue_after()
            consume(buf1, i + 1)

        def body(t, carry):
            pair(2 * t, lambda: issue(buf0, 2 * t + 2))
            return carry

        lax.fori_loop(0, n // 2 - 1, body, 0, unroll=4)
        pair(n - 2, issue_next_phase)

    tri = (lax.broadcasted_iota(jnp.int32, (half, tq), 0)
           <= (lax.broadcasted_iota(jnp.int32, (half, tq), 1) & (half - 1)))

    def diag_scores(buf, i):
        s_ref, cmax_ref = buf
        r0 = pl.multiple_of(i * tk, tk)
        s_a = _dot(k_ref[0, pl.ds(r0, half), :], qbd_ref[i])
        s_b = _dot(k_ref[0, pl.ds(r0 + half, half), :], qbd_ref[i, :, tq:2 * tq])
        left = jnp.where(tri, s_a[:, 0:tq], NEG_BIG)
        right = s_a[:, tq:2 * tq]
        s_b = jnp.where(tri, s_b, NEG_BIG)
        s_ref[0:half, 0:tq] = left
        s_ref[0:half, tq:2 * tq] = right
        s_ref[half:tq, tq:2 * tq] = s_b
        cmax_ref[:, 0:tq] = jnp.max(left, axis=0, keepdims=True)
        cmax_ref[:, tq:2 * tq] = jnp.maximum(jnp.max(right, axis=0, keepdims=True),
                                             jnp.max(s_b, axis=0, keepdims=True))

    def diag_softmax_pv(buf, i):
        s_ref, cmax_ref = buf
        m_new = cmax_ref[...]
        p_a = jnp.exp2(s_ref[0:half, :] - m_new).astype(BF16)
        p_b = jnp.exp2(s_ref[half:tq, tq:2 * tq] - m_new[:, tq:2 * tq]).astype(BF16)
        pv_a = _dot(vt_ref[i, :, 0:half], p_a)
        pv_b = _dot(vt_ref[i, :, half:tq], p_b)
        acc_ref[i, :, 0:tq] = pv_a[:, 0:tq]
        acc_ref[i, :, tq:2 * tq] = pv_a[:, tq:2 * tq] + pv_b
        m_ref[i] = m_new

    def scores(buf, i):
        s_ref, cmax_ref = buf
        k = k_ref[0, pl.ds(pl.multiple_of(vk_ref[i] * tk, tk), tk), :]
        s = _dot(k, qbd_ref[vq_ref[i]])
        s_ref[...] = s
        cmax_ref[...] = jnp.max(s, axis=0, keepdims=True)

    def softmax_pv(buf, i):
        s_ref, cmax_ref = buf
        qi = vq_ref[i]
        m = m_ref[qi]
        m_new = jnp.maximum(m, cmax_ref[...])
        alpha = jnp.exp2(m - m_new)
        p = jnp.exp2(s_ref[...] - m_new).astype(BF16)
        acc_ref[qi] = alpha * acc_ref[qi] + _dot(vt_ref[vk_ref[i]], p)
        m_ref[qi] = m_new

    diag_scores(buf0, 0)
    pipelined(nq, diag_scores, diag_softmax_pv, lambda: scores(buf0, 0))
    pipelined(n_off, scores, softmax_pv, lambda: None)

    gain_col = jnp.broadcast_to(g_ref[...] * (1.0 - lam_init), (8, LANES)).T[:, 0:1]
    for i in range(nq):
        acc = acc_ref[i]
        o_t = acc[0:LANES, :] / acc[LANES:LANES + 1, :]
        o1 = jnp.concatenate([o_t[:, 0:half], o_t[:, tq:tq + half]], axis=1)
        o2 = jnp.concatenate([o_t[:, half:tq], o_t[:, tq + half:2 * tq]], axis=1)
        d_t = o1 - lam * o2
        rinv = lax.rsqrt(jnp.mean(d_t * d_t, axis=0, keepdims=True) + EPS)
        o = (d_t * rinv * gain_col).T
        rows = slice(i * tq, (i + 1) * tq)
        o_ref[0, rows, :] = (o * _silu(z_ref[0, rows, :].astype(F32))).astype(BF16)


def _diff_attn(p3, lam_vecs, head_g, *, tq=512):
    b, s, _ = p3.shape
    nq = s // tq
    hq, hk, hv, hz = (off // LANES for off in (_OFF_AQ, _OFF_AK, _OFF_AV, _OFF_AZ))
    lam_init = 0.8 - 0.6 * math.exp(-0.3 * 0)
    below = [(qi, kb) for qi in range(nq) for kb in range(qi)]
    assert nq % 2 == 0 and len(below) % 2 == 0
    visit_q = jnp.asarray([qi for qi, _ in below], jnp.int32)
    visit_k = jnp.asarray([kb for _, kb in below], jnp.int32)
    seq = lambda col0: pl.BlockSpec((1, s, LANES), lambda bi, h, vq, vk: (bi, 0, col0 + h))
    return pl.pallas_call(
        functools.partial(_diff_attn_kernel, tq=tq, lam_init=lam_init),
        grid_spec=pltpu.PrefetchScalarGridSpec(
            num_scalar_prefetch=2,
            grid=(b, DA_HEADS),
            in_specs=[
                pl.BlockSpec((4, DA_QK_DIM), lambda bi, h, vq, vk: (0, 0)),
                seq(hq), seq(hk), seq(hv), seq(hz),
                pl.BlockSpec((1, DA_V_DIM), lambda bi, h, vq, vk: (0, 0)),
            ],
            out_specs=pl.BlockSpec((1, s, LANES), lambda bi, h, vq, vk: (bi, 0, h)),
            scratch_shapes=[
                pltpu.VMEM((nq, LANES + SUM_ROWS, tq), BF16),
                pltpu.VMEM((nq, LANES, 2 * tq), BF16),
                pltpu.VMEM((nq, LANES + SUM_ROWS, 2 * tq), F32),
                pltpu.VMEM((nq, 1, 2 * tq), F32),
                pltpu.VMEM((tq, 2 * tq), F32),
                pltpu.VMEM((tq, 2 * tq), F32),
                pltpu.VMEM((1, 2 * tq), F32),
                pltpu.VMEM((1, 2 * tq), F32),
            ],
        ),
        out_shape=jax.ShapeDtypeStruct((b, s, DA_WIDTH), BF16),
        compiler_params=pltpu.CompilerParams(
            dimension_semantics=("arbitrary", "arbitrary"), vmem_limit_bytes=VMEM_LIMIT),
        name="diff_attn",
    )(visit_q, visit_k, lam_vecs, p3, p3, p3, p3, head_g)


def _mlstm_kernel(qp_ref, kp_ref, v_ref, o_ref, z_ref, gt_ref, cwq_ref, cwk_ref, cbq_ref, cbk_ref,
                  hg_ref, y_ref, xq_ref, xk_ref, ic_ref, bc_ref, c_ref):
    h = pl.program_id(1)
    s = qp_ref.shape[1]
    L = ML_CHUNK
    nc = s // L
    pad = 8

    zeros = jnp.zeros((pad, ML_QK_DIM), F32)
    xq_ref[0:pad, :] = zeros
    xk_ref[0:pad, :] = zeros
    xq_ref[pad:, :] = qp_ref[0].astype(F32)
    xk_ref[pad:, :] = kp_ref[0].astype(F32)

    ic = gt_ref[h]
    fp = gt_ref[ML_HEADS + h]
    lf = jnp.minimum(fp, 0.0) - jnp.log(1.0 + jnp.exp(-jnp.abs(fp)))
    r_i = lax.broadcasted_iota(jnp.int32, (L, L), 0)
    c_i = lax.broadcasted_iota(jnp.int32, (L, L), 1)
    tri_u = (r_i <= c_i).astype(BF16)
    lf_hi = lf.astype(BF16)
    lf_lo = (lf - lf_hi.astype(F32)).astype(BF16)
    ic_ref[...] = ic
    bc_ref[...] = _dot(lf_hi, tri_u) + _dot(lf_lo, tri_u)

    c_ref[...] = jnp.zeros_like(c_ref)
    causal = c_i <= r_i
    ones_col = (lax.broadcasted_iota(jnp.int32, (L, LANES), 1) == 0).astype(BF16)
    cwq = cwq_ref[...]
    cwk = cwk_ref[...]

    def conv(x_ref, r0, w, bias):
        xw = x_ref[pl.ds(r0, L + pad), :]
        acc = bias
        for j in range(CONV_WIDTH):
            lo = pad - (CONV_WIDTH - 1) + j
            acc = acc + w[j:j + 1, :] * xw[lo:lo + L, :]
        return _silu(acc)

    def chunk(c, m_prev):
        r0 = pl.multiple_of(c * L, L)
        q = (conv(xq_ref, r0, cwq, cbq_ref[...]) * ML_QK_DIM ** -0.5).astype(BF16)
        k_t = conv(xk_ref, r0, cwk, cbk_ref[...]).T
        v_ext = jnp.concatenate([v_ref[0, pl.ds(r0, L), :], ones_col], axis=1)

        ic_row = ic_ref[pl.ds(c, 1), :]
        bc_row = bc_ref[pl.ds(c, 1), :]
        bc_col = jnp.broadcast_to(bc_row, (8, L)).T[:, 0:1]
        b_last = bc_row[:, L - 1:L]

        log_d = jnp.where(causal, bc_col - bc_row + ic_row, NEG_BIG)
        log_inter = bc_col + m_prev
        m_t = jnp.maximum(log_inter, jnp.max(log_d, axis=-1, keepdims=True))
        dmat = jnp.exp(log_d - m_t)
        inter_w = jnp.exp(log_inter - m_t)
        qk = (_dot(q, k_t.astype(BF16)) * dmat).astype(BF16)
        nd = _dot(qk, v_ext) + inter_w * _dot(q, c_ref[...].astype(BF16))
        num = nd[:, 0:ML_V_DIM]
        den = nd[:, ML_V_DIM:ML_V_DIM + 1]
        hm = num / jnp.maximum(jnp.abs(den), jnp.exp(-m_t))
        hm = hm * _sigmoid(o_ref[0, pl.ds(r0, L), :].astype(F32))
        hm = hm * _rms_scale(hm) * hg_ref[...]
        y_ref[0, pl.ds(r0, L), :] = (hm * _silu(z_ref[0, pl.ds(r0, L), :].astype(F32))).astype(BF16)

        g_row = b_last - bc_row + ic_row
        m_new = jnp.maximum(b_last + m_prev, jnp.max(g_row, axis=-1, keepdims=True))
        decay = jnp.exp(b_last + m_prev - m_new)
        w_row = jnp.exp(g_row - m_new)
        c_ref[...] = decay * c_ref[...] + _dot((k_t * w_row).astype(BF16), v_ext)
        return m_new

    lax.fori_loop(0, nc, chunk, jnp.zeros((1, 1), F32), unroll=8)


def _mlstm(p3, gates_t, conv_w, conv_b, head_g):
    b, s, _ = p3.shape
    nc = s // ML_CHUNK
    cq = _OFF_BQK // ML_QK_DIM
    ck = cq + ML_HEADS
    cv, co, cz = (off // ML_V_DIM for off in (5120, 6144, 7168))
    seq = lambda width, col0: pl.BlockSpec((1, s, width), lambda bi, h: (bi, 0, col0 + h))
    return pl.pallas_call(
        _mlstm_kernel,
        grid=(b, ML_HEADS),
        in_specs=[
            seq(ML_QK_DIM, cq), seq(ML_QK_DIM, ck), seq(ML_V_DIM, cv), seq(ML_V_DIM, co), seq(ML_V_DIM, cz),
            pl.BlockSpec((2 * ML_HEADS, nc, ML_CHUNK), lambda bi, h: (0, bi, 0)),
            pl.BlockSpec((CONV_WIDTH, ML_QK_DIM), lambda bi, h: (0, h)),
            pl.BlockSpec((CONV_WIDTH, ML_QK_DIM), lambda bi, h: (0, ML_HEADS + h)),
            pl.BlockSpec((1, ML_QK_DIM), lambda bi, h: (0, h)),
            pl.BlockSpec((1, ML_QK_DIM), lambda bi, h: (0, ML_HEADS + h)),
            pl.BlockSpec((1, ML_V_DIM), lambda bi, h: (0, 0)),
        ],
        out_specs=pl.BlockSpec((1, s, ML_V_DIM), lambda bi, h: (bi, 0, h)),
        out_shape=jax.ShapeDtypeStruct((b, s, ML_WIDTH), BF16),
        scratch_shapes=[
            pltpu.VMEM((s + 8, ML_QK_DIM), F32),
            pltpu.VMEM((s + 8, ML_QK_DIM), F32),
            pltpu.VMEM((nc, ML_CHUNK), F32),
            pltpu.VMEM((nc, ML_CHUNK), F32),
            pltpu.VMEM((ML_QK_DIM, ML_V_DIM + LANES), F32),
        ],
        compiler_params=pltpu.CompilerParams(
            dimension_semantics=("arbitrary", "arbitrary"), vmem_limit_bytes=VMEM_LIMIT),
        name="mlstm",
    )(p3, p3, p3, p3, p3, gates_t, conv_w, conv_w, conv_b, conv_b, head_g)


def _l0_out_proj_kernel(ya_ref, yb_ref, x_ref, wa_ref, wb_ref, g_ref, h_ref):
    y = _dot(ya_ref[...], wa_ref[...]) + _dot(yb_ref[...], wb_ref[...])
    h_ref[...] = x_ref[...] + y * _rms_scale(y) * g_ref[...]


def _l0_out_proj(ya, yb, x2, w_out, post_g, *, tm=1024):
    m = x2.shape[0]
    assert DA_WIDTH == ML_WIDTH
    row = lambda width: pl.BlockSpec((tm, width), lambda i: (i, 0))
    w_rows = lambda blk: pl.BlockSpec((DA_WIDTH, D_MODEL), lambda i: (blk, 0))
    return pl.pallas_call(
        _l0_out_proj_kernel,
        grid=(m // tm,),
        in_specs=[row(DA_WIDTH), row(ML_WIDTH), row(D_MODEL),
                  w_rows(0), w_rows(1), pl.BlockSpec((1, D_MODEL), lambda i: (0, 0))],
        out_specs=row(D_MODEL),
        out_shape=jax.ShapeDtypeStruct((m, D_MODEL), F32),
        compiler_params=pltpu.CompilerParams(
            dimension_semantics=("arbitrary",), vmem_limit_bytes=VMEM_LIMIT),
        name="l0_out_proj",
    )(ya, yb, x2, w_out, w_out, post_g)


def _l1_gmlp_kernel(h_ref, pre_g_ref, wu_ref, wv_ref, wz_ref, sg_g_ref, wsp_ref, bsp_ref,
                    wo_ref, post_g_ref, o_ref, vn_ref, y_ref, *, tm):
    hres = h_ref[...]
    hn = (hres * _rms_scale(hres) * pre_g_ref[...]).astype(BF16)

    v = _gelu_tanh(_dot(hn, wv_ref[...]))
    vn_ref[...] = (v * _rms_scale(v) * sg_g_ref[...]).astype(BF16)

    r_i = lax.broadcasted_iota(jnp.int32, (SG_CHUNK, SG_CHUNK), 0)
    c_i = lax.broadcasted_iota(jnp.int32, (SG_CHUNK, SG_CHUNK), 1)
    causal = c_i <= r_i
    for g in range(SG_GROUPS):
        cols = slice(g * SG_GROUP_DIM, (g + 1) * SG_GROUP_DIM)
        u = _gelu_tanh(_dot(hn, wu_ref[:, cols]))
        gate = _silu(_dot(hn, wz_ref[:, cols]))
        wm = jnp.where(causal, wsp_ref[g], 0.0).astype(BF16)
        bias = bsp_ref[g]
        for c in range(tm // SG_CHUNK):
            rows = slice(c * SG_CHUNK, (c + 1) * SG_CHUNK)
            vs = _dot(wm, vn_ref[rows, cols]) + bias
            y_ref[rows, cols] = (u[rows] * vs * gate[rows]).astype(BF16)

    out = _dot(y_ref[...], wo_ref[...])
    o_ref[...] = hres + out * _rms_scale(out) * post_g_ref[...]


def _l1_gmlp(h1, pre_g, w_in, sg_g, w_spatial, b_spatial, wo, post_g, *, tm=1024):
    m = h1.shape[0]
    row = pl.BlockSpec((tm, D_MODEL), lambda i: (i, 0))
    const2 = lambda shape: pl.BlockSpec(shape, lambda i: (0, 0), pipeline_mode=pl.Buffered(1))
    const3 = lambda shape: pl.BlockSpec(shape, lambda i: (0, 0, 0), pipeline_mode=pl.Buffered(1))
    w_cols = lambda blk: pl.BlockSpec((D_MODEL, SG_WIDTH), lambda i: (0, blk), pipeline_mode=pl.Buffered(1))
    return pl.pallas_call(
        functools.partial(_l1_gmlp_kernel, tm=tm),
        grid=(m // tm,),
        in_specs=[
            row, const2((1, D_MODEL)),
            w_cols(0), w_cols(1), w_cols(2),
            const2((1, SG_WIDTH)),
            const3((SG_GROUPS, SG_CHUNK, SG_CHUNK)), const3((SG_GROUPS, SG_CHUNK, 1)),
            const2((SG_WIDTH, D_MODEL)), const2((1, D_MODEL)),
        ],
        out_specs=row,
        out_shape=jax.ShapeDtypeStruct((m, D_MODEL), F32),
        scratch_shapes=[pltpu.VMEM((tm, SG_WIDTH), BF16), pltpu.VMEM((tm, SG_WIDTH), BF16)],
        compiler_params=pltpu.CompilerParams(
            dimension_semantics=("arbitrary",), vmem_limit_bytes=VMEM_LIMIT),
        name="l1_gmlp",
    )(h1, pre_g, w_in, w_in, w_in, sg_g, w_spatial, b_spatial, wo, post_g)


def kernel(x, l0_pre_g, l0_w_in, l0_b_igate, l0_b_fgate, l0_conv_w, l0_conv_b, l0_lambda_q1, l0_lambda_k1, l0_lambda_q2, l0_lambda_k2, l0_da_head_g, l0_ml_head_g, l0_w_out, l0_post_g, l1_pre_g, l1_w_in, l1_sg_norm_g, l1_w_spatial, l1_b_spatial, l1_w_out, l1_post_g):
    b, s, d = x.shape
    m = b * s
    x2 = x.reshape(m, d)
    row = lambda v: v.reshape(1, -1).astype(F32)

    w_in0 = l0_w_in.astype(BF16)
    w_main = jnp.concatenate([w_in0[:, :_OFF_BI], w_in0[:, _OFF_BO:]], axis=1)
    w_gate = jnp.pad(w_in0[:, _OFF_BI:_OFF_BO], ((0, 0), (0, LANES - 2 * ML_HEADS)))
    b_gate = jnp.pad(jnp.concatenate([l0_b_igate, l0_b_fgate]), (0, LANES - 2 * ML_HEADS)).reshape(1, LANES)
    lam_vecs = jnp.stack([l0_lambda_q1, l0_lambda_k1, l0_lambda_q2, l0_lambda_k2]).astype(F32)

    p, gates_t = _l0_in_proj(x2, row(l0_pre_g), w_main, w_gate, b_gate.astype(F32))
    p3 = p.reshape(b, s, L0_MAIN)
    gates_t = gates_t.reshape(2 * ML_HEADS, m // ML_CHUNK, ML_CHUNK)

    y_a = _diff_attn(p3, lam_vecs, row(l0_da_head_g))
    y_b = _mlstm(p3, gates_t, l0_conv_w.astype(F32), row(l0_conv_b), row(l0_ml_head_g))

    h1 = _l0_out_proj(y_a.reshape(m, DA_WIDTH), y_b.reshape(m, ML_WIDTH), x2,
                      l0_w_out.astype(BF16), row(l0_post_g))

    h2 = _l1_gmlp(h1, row(l1_pre_g), l1_w_in.astype(BF16),
                  row(l1_sg_norm_g), l1_w_spatial.astype(F32),
                  l1_b_spatial.astype(F32).reshape(SG_GROUPS, SG_CHUNK, 1),
                  l1_w_out.astype(BF16), row(l1_post_g))
    return h2.reshape(b, s, d)
```

```python
import functools
import math

import jax
import jax.numpy as jnp
from jax import lax
from jax.experimental import pallas as pl
from jax.experimental.pallas import tpu as pltpu

F32 = jnp.float32
BF16 = jnp.bfloat16

D_MODEL = 1024
EPS = 1e-6
DA_HEADS = 8
DA_QK_DIM = 64
DA_V_DIM = 128
DA_WIDTH = DA_HEADS * DA_V_DIM
ML_HEADS = 4
ML_QK_DIM = 128
ML_V_DIM = 256
ML_WIDTH = ML_HEADS * ML_V_DIM
ML_CHUNK = 128
CONV_WIDTH = 4
SG_GROUPS = 8
SG_CHUNK = 128
SG_WIDTH = 2 * D_MODEL
SG_GROUP_DIM = SG_WIDTH // SG_GROUPS

LANES = 128
SUM_ROWS = 16
VMEM_LIMIT = 56 * 1024 * 1024
NEG_BIG = -1e30
LOG2E = 1.4426950408889634

_OFF_AQ, _OFF_AK, _OFF_AV, _OFF_AZ = 0, 1024, 2048, 3072
_OFF_BQK, _OFF_BV, _OFF_BI, _OFF_BF, _OFF_BO, _OFF_BZ = 4096, 5120, 6144, 6148, 6152, 7176
L0_MAIN = 8192


def _sigmoid(x):
    return 1.0 / (1.0 + jnp.exp2(x * -LOG2E))


def _silu(x):
    return x * _sigmoid(x)


def _gelu_tanh(x):
    c = math.sqrt(2.0 / math.pi)
    return 0.5 * x * (1.0 + jnp.tanh(c * (x + 0.044715 * (x * x * x))))


def _rms_scale(x):
    return lax.rsqrt(jnp.mean(x * x, axis=-1, keepdims=True) + EPS)


def _dot(a, b):
    return jnp.dot(a, b, preferred_element_type=F32)


def _l0_in_proj_kernel(x_ref, g_ref, w_ref, cs_ref, wg_ref, bg_ref, p_ref, gt_ref, hn_ref):
    j = pl.program_id(1)

    @pl.when(j == 0)
    def _():
        x = x_ref[...]
        hn = (x * _rms_scale(x) * g_ref[...]).astype(BF16)
        hn_ref[...] = hn
        gates = _dot(hn, wg_ref[...]) + bg_ref[...]
        gt_ref[...] = gates.T[0:8, :]

    p_ref[...] = (_dot(hn_ref[...], w_ref[...]) * cs_ref[...]).astype(BF16)


def _l0_in_proj(x2, pre_g, w_main, w_gate, b_gate, *, tm=2048, tn=2048):
    m = x2.shape[0]
    q_scale = DA_QK_DIM ** -0.5 * LOG2E
    col_scale = jnp.where(jnp.arange(L0_MAIN) < DA_HEADS * 2 * DA_QK_DIM, q_scale, 1.0).astype(F32).reshape(1, -1)
    return pl.pallas_call(
        _l0_in_proj_kernel,
        grid=(m // tm, L0_MAIN // tn),
        in_specs=[
            pl.BlockSpec((tm, D_MODEL), lambda i, j: (i, 0)),
            pl.BlockSpec((1, D_MODEL), lambda i, j: (0, 0)),
            pl.BlockSpec((D_MODEL, tn), lambda i, j: (0, j)),
            pl.BlockSpec((1, tn), lambda i, j: (0, j)),
            pl.BlockSpec((D_MODEL, LANES), lambda i, j: (0, 0)),
            pl.BlockSpec((1, LANES), lambda i, j: (0, 0)),
        ],
        out_specs=[
            pl.BlockSpec((tm, tn), lambda i, j: (i, j)),
            pl.BlockSpec((8, tm), lambda i, j: (0, i)),
        ],
        out_shape=[
            jax.ShapeDtypeStruct((m, L0_MAIN), BF16),
            jax.ShapeDtypeStruct((8, m), F32),
        ],
        scratch_shapes=[pltpu.VMEM((tm, D_MODEL), BF16)],
        compiler_params=pltpu.CompilerParams(
            dimension_semantics=("arbitrary", "arbitrary"), vmem_limit_bytes=VMEM_LIMIT),
        name="l0_in_proj",
    )(x2, pre_g, w_main, col_scale, w_gate, b_gate)


def _diff_attn_kernel(vq_ref, vk_ref, lam_ref, q_ref, k_ref, v_ref, z_ref, g_ref, o_ref,
                      vt_ref, qbd_ref, acc_ref, m_ref, s0_ref, s1_ref, cmax0_ref, cmax1_ref, *, tq, lam_init):
    nq = vt_ref.shape[0]
    tk = tq
    n_off = vq_ref.shape[0]

    lam_v = lam_ref[...]
    lam = (jnp.exp(jnp.sum(lam_v[0:1] * lam_v[1:2], axis=-1, keepdims=True))
           - jnp.exp(jnp.sum(lam_v[2:3] * lam_v[3:4], axis=-1, keepdims=True)) + lam_init)

    half = tq // 2
    ones_row = (lax.broadcasted_iota(jnp.int32, (SUM_ROWS, tk), 0) == 0).astype(BF16)
    first = lax.broadcasted_iota(jnp.int32, (LANES, half), 0) < DA_QK_DIM
    zero = jnp.zeros((LANES, half), BF16)
    for i in range(nq):
        rows = slice(i * tq, (i + 1) * tq)
        vt_ref[i, 0:LANES, :] = v_ref[0, rows, :].T
        vt_ref[i, LANES:LANES + SUM_ROWS, :] = ones_row
        qt = q_ref[0, rows, :].T
        parts = []
        for qh in (qt[:, 0:half], qt[:, half:tq]):
            parts += [jnp.where(first, qh, zero), jnp.where(first, zero, qh)]
        qbd_ref[i] = jnp.concatenate(parts, axis=1)

    buf0, buf1 = (s0_ref, cmax0_ref), (s1_ref, cmax1_ref)

    def pipelined(n, issue, consume, issue_next_phase):
        def pair(i, issue_after):
            issue(buf1, i + 1)
            consume(buf0, i)
            issue_after()
            consume(buf1, i + 1)

        def body(t, carry):
            pair(2 * t, lambda: issue(buf0, 2 * t + 2))
            return carry

        lax.fori_loop(0, n // 2 - 1, body, 0, unroll=4)
        pair(n - 2, issue_next_phase)

    tri = (lax.broadcasted_iota(jnp.int32, (half, tq), 0)
           <= (lax.broadcasted_iota(jnp.int32, (half, tq), 1) & (half - 1)))

    def diag_scores(buf, i):
        s_ref, cmax_ref = buf
        r0 = pl.multiple_of(i * tk, tk)
        s_a = _dot(k_ref[0, pl.ds(r0, half), :], qbd_ref[i])
        s_b = _dot(k_ref[0, pl.ds(r0 + half, half), :], qbd_ref[i, :, tq:2 * tq])
        left = jnp.where(tri, s_a[:, 0:tq], NEG_BIG)
        right = s_a[:, tq:2 * tq]
        s_b = jnp.where(tri, s_b, NEG_BIG)
        s_ref[0:half, 0:tq] = left
        s_ref[0:half, tq:2 * tq] = right
        s_ref[half:tq, tq:2 * tq] = s_b
        cmax_ref[:, 0:tq] = jnp.max(left, axis=0, keepdims=True)
        cmax_ref[:, tq:2 * tq] = jnp.maximum(jnp.max(right, axis=0, keepdims=True),
                                             jnp.max(s_b, axis=0, keepdims=True))

    def diag_softmax_pv(buf, i):
        s_ref, cmax_ref = buf
        m_new = cmax_ref[...]
        p_a = jnp.exp2(s_ref[0:half, :] - m_new).astype(BF16)
        p_b = jnp.exp2(s_ref[half:tq, tq:2 * tq] - m_new[:, tq:2 * tq]).astype(BF16)
        pv_a = _dot(vt_ref[i, :, 0:half], p_a)
        pv_b = _dot(vt_ref[i, :, half:tq], p_b)
        acc_ref[i, :, 0:tq] = pv_a[:, 0:tq]
        acc_ref[i, :, tq:2 * tq] = pv_a[:, tq:2 * tq] + pv_b
        m_ref[i] = m_new

    def scores(buf, i):
        s_ref, cmax_ref = buf
        k = k_ref[0, pl.ds(pl.multiple_of(vk_ref[i] * tk, tk), tk), :]
        s = _dot(k, qbd_ref[vq_ref[i]])
        s_ref[...] = s
        cmax_ref[...] = jnp.max(s, axis=0, keepdims=True)

    def softmax_pv(buf, i):
        s_ref, cmax_ref = buf
        qi = vq_ref[i]
        m = m_ref[qi]
        m_new = jnp.maximum(m, cmax_ref[...])
        alpha = jnp.exp2(m - m_new)
        p = jnp.exp2(s_ref[...] - m_new).astype(BF16)
        acc_ref[qi] = alpha * acc_ref[qi] + _dot(vt_ref[vk_ref[i]], p)
        m_ref[qi] = m_new

    diag_scores(buf0, 0)
    pipelined(nq, diag_scores, diag_softmax_pv, lambda: scores(buf0, 0))
    pipelined(n_off, scores, softmax_pv, lambda: None)

    gain_col = jnp.broadcast_to(g_ref[...] * (1.0 - lam_init), (8, LANES)).T[:, 0:1]
    for i in range(nq):
        acc = acc_ref[i]
        o_t = acc[0:LANES, :] / acc[LANES:LANES + 1, :]
        o1 = jnp.concatenate([o_t[:, 0:half], o_t[:, tq:tq + half]], axis=1)
        o2 = jnp.concatenate([o_t[:, half:tq], o_t[:, tq + half:2 * tq]], axis=1)
        d_t = o1 - lam * o2
        rinv = lax.rsqrt(jnp.mean(d_t * d_t, axis=0, keepdims=True) + EPS)
        o = (d_t * rinv * gain_col).T
        rows = slice(i * tq, (i + 1) * tq)
        o_ref[0, rows, :] = (o * _silu(z_ref[0, rows, :].astype(F32))).astype(BF16)


def _diff_attn(p3, lam_vecs, head_g, *, tq=512):
    b, s, _ = p3.shape
    nq = s // tq
    hq, hk, hv, hz = (off // LANES for off in (_OFF_AQ, _OFF_AK, _OFF_AV, _OFF_AZ))
    layer = 0
    lam_init = 0.8 - 0.6 * math.exp(-0.3 * layer)
    below = [(qi, kb) for qi in range(nq) for kb in range(qi)]
    assert nq % 2 == 0 and len(below) % 2 == 0
    visit_q = jnp.asarray([qi for qi, _ in below], jnp.int32)
    visit_k = jnp.asarray([kb for _, kb in below], jnp.int32)
    seq = lambda col0: pl.BlockSpec((1, s, LANES), lambda bi, h, vq, vk: (bi, 0, col0 + h))
    return pl.pallas_call(
        functools.partial(_diff_attn_kernel, tq=tq, lam_init=lam_init),
        grid_spec=pltpu.PrefetchScalarGridSpec(
            num_scalar_prefetch=2,
            grid=(b, DA_HEADS),
            in_specs=[
                pl.BlockSpec((4, DA_QK_DIM), lambda bi, h, vq, vk: (0, 0)),
                seq(hq), seq(hk), seq(hv), seq(hz),
                pl.BlockSpec((1, DA_V_DIM), lambda bi, h, vq, vk: (0, 0)),
            ],
            out_specs=pl.BlockSpec((1, s, LANES), lambda bi, h, vq, vk: (bi, 0, h)),
            scratch_shapes=[
                pltpu.VMEM((nq, LANES + SUM_ROWS, tq), BF16),
                pltpu.VMEM((nq, LANES, 2 * tq), BF16),
                pltpu.VMEM((nq, LANES + SUM_ROWS, 2 * tq), F32),
                pltpu.VMEM((nq, 1, 2 * tq), F32),
                pltpu.VMEM((tq, 2 * tq), F32),
                pltpu.VMEM((tq, 2 * tq), F32),
                pltpu.VMEM((1, 2 * tq), F32),
                pltpu.VMEM((1, 2 * tq), F32),
            ],
        ),
        out_shape=jax.ShapeDtypeStruct((b, s, DA_WIDTH), BF16),
        compiler_params=pltpu.CompilerParams(
            dimension_semantics=("arbitrary", "arbitrary"), vmem_limit_bytes=VMEM_LIMIT),
        name="diff_attn",
    )(visit_q, visit_k, lam_vecs, p3, p3, p3, p3, head_g)


def _mlstm_kernel(qp_ref, kp_ref, v_ref, o_ref, z_ref, gt_ref, cwq_ref, cwk_ref, cbq_ref, cbk_ref,
                  hg_ref, y_ref, xq_ref, xk_ref, ic_ref, bc_ref, c_ref):
    h = pl.program_id(1)
    s = qp_ref.shape[1]
    L = ML_CHUNK
    nc = s // L
    pad = 8

    zeros = jnp.zeros((pad, ML_QK_DIM), F32)
    xq_ref[0:pad, :] = zeros
    xk_ref[0:pad, :] = zeros
    xq_ref[pad:, :] = qp_ref[0].astype(F32)
    xk_ref[pad:, :] = kp_ref[0].astype(F32)

    ic = gt_ref[h]
    fp = gt_ref[ML_HEADS + h]
    lf = jnp.minimum(fp, 0.0) - jnp.log(1.0 + jnp.exp(-jnp.abs(fp)))
    r_i = lax.broadcasted_iota(jnp.int32, (L, L), 0)
    c_i = lax.broadcasted_iota(jnp.int32, (L, L), 1)
    tri_u = (r_i <= c_i).astype(BF16)
    lf_hi = lf.astype(BF16)
    lf_lo = (lf - lf_hi.astype(F32)).astype(BF16)
    ic_ref[...] = ic
    bc_ref[...] = _dot(lf_hi, tri_u) + _dot(lf_lo, tri_u)

    c_ref[...] = jnp.zeros_like(c_ref)
    causal = c_i <= r_i
    ones_col = (lax.broadcasted_iota(jnp.int32, (L, LANES), 1) == 0).astype(BF16)
    cwq = cwq_ref[...]
    cwk = cwk_ref[...]

    def conv(x_ref, r0, w, bias):
        xw = x_ref[pl.ds(r0, L + pad), :]
        acc = bias
        for j in range(CONV_WIDTH):
            lo = pad - (CONV_WIDTH - 1) + j
            acc = acc + w[j:j + 1, :] * xw[lo:lo + L, :]
        return _silu(acc)

    def chunk(c, m_prev):
        r0 = pl.multiple_of(c * L, L)
        q = (conv(xq_ref, r0, cwq, cbq_ref[...]) * ML_QK_DIM ** -0.5).astype(BF16)
        k_t = conv(xk_ref, r0, cwk, cbk_ref[...]).T
        v_ext = jnp.concatenate([v_ref[0, pl.ds(r0, L), :], ones_col], axis=1)

        ic_row = ic_ref[pl.ds(c, 1), :]
        bc_row = bc_ref[pl.ds(c, 1), :]
        bc_col = jnp.broadcast_to(bc_row, (8, L)).T[:, 0:1]
        b_last = bc_row[:, L - 1:L]

        log_d = jnp.where(causal, bc_col - bc_row + ic_row, NEG_BIG)
        log_inter = bc_col + m_prev
        m_t = jnp.maximum(log_inter, jnp.max(log_d, axis=-1, keepdims=True))
        dmat = jnp.exp(log_d - m_t)
        inter_w = jnp.exp(log_inter - m_t)
        qk = (_dot(q, k_t.astype(BF16)) * dmat).astype(BF16)
        nd = _dot(qk, v_ext) + inter_w * _dot(q, c_ref[...].astype(BF16))
        num = nd[:, 0:ML_V_DIM]
        den = nd[:, ML_V_DIM:ML_V_DIM + 1]
        hm = num / jnp.maximum(jnp.abs(den), jnp.exp(-m_t))
        hm = hm * _sigmoid(o_ref[0, pl.ds(r0, L), :].astype(F32))
        hm = hm * _rms_scale(hm) * hg_ref[...]
        y_ref[0, pl.ds(r0, L), :] = (hm * _silu(z_ref[0, pl.ds(r0, L), :].astype(F32))).astype(BF16)

        g_row = b_last - bc_row + ic_row
        m_new = jnp.maximum(b_last + m_prev, jnp.max(g_row, axis=-1, keepdims=True))
        decay = jnp.exp(b_last + m_prev - m_new)
        w_row = jnp.exp(g_row - m_new)
        c_ref[...] = decay * c_ref[...] + _dot((k_t * w_row).astype(BF16), v_ext)
        return m_new

    lax.fori_loop(0, nc, chunk, jnp.zeros((1, 1), F32), unroll=8)


def _mlstm(p3, gates_t, conv_w, conv_b, head_g):
    b, s, _ = p3.shape
    nc = s // ML_CHUNK
    cq = _OFF_BQK // ML_QK_DIM
    ck = cq + ML_HEADS
    n_gate = _OFF_BO - _OFF_BI
    cv, co, cz = (off // ML_V_DIM for off in (_OFF_BV, _OFF_BO - n_gate, _OFF_BZ - n_gate))
    seq = lambda width, col0: pl.BlockSpec((1, s, width), lambda bi, h: (bi, 0, col0 + h))
    return pl.pallas_call(
        _mlstm_kernel,
        grid=(b, ML_HEADS),
        in_specs=[
            seq(ML_QK_DIM, cq), seq(ML_QK_DIM, ck), seq(ML_V_DIM, cv), seq(ML_V_DIM, co), seq(ML_V_DIM, cz),
            pl.BlockSpec((2 * ML_HEADS, nc, ML_CHUNK), lambda bi, h: (0, bi, 0)),
            pl.BlockSpec((CONV_WIDTH, ML_QK_DIM), lambda bi, h: (0, h)),
            pl.BlockSpec((CONV_WIDTH, ML_QK_DIM), lambda bi, h: (0, ML_HEADS + h)),
            pl.BlockSpec((1, ML_QK_DIM), lambda bi, h: (0, h)),
            pl.BlockSpec((1, ML_QK_DIM), lambda bi, h: (0, ML_HEADS + h)),
            pl.BlockSpec((1, ML_V_DIM), lambda bi, h: (0, 0)),
        ],
        out_specs=pl.BlockSpec((1, s, ML_V_DIM), lambda bi, h: (bi, 0, h)),
        out_shape=jax.ShapeDtypeStruct((b, s, ML_WIDTH), BF16),
        scratch_shapes=[
            pltpu.VMEM((s + 8, ML_QK_DIM), F32),
            pltpu.VMEM((s + 8, ML_QK_DIM), F32),
            pltpu.VMEM((nc, ML_CHUNK), F32),
            pltpu.VMEM((nc, ML_CHUNK), F32),
            pltpu.VMEM((ML_QK_DIM, ML_V_DIM + LANES), F32),
        ],
        compiler_params=pltpu.CompilerParams(
            dimension_semantics=("arbitrary", "arbitrary"), vmem_limit_bytes=VMEM_LIMIT),
        name="mlstm",
    )(p3, p3, p3, p3, p3, gates_t, conv_w, conv_w, conv_b, conv_b, head_g)


def _l0_out_proj_kernel(ya_ref, yb_ref, x_ref, wa_ref, wb_ref, g_ref, h_ref):
    y = _dot(ya_ref[...], wa_ref[...]) + _dot(yb_ref[...], wb_ref[...])
    h_ref[...] = x_ref[...] + y * _rms_scale(y) * g_ref[...]


def _l0_out_proj(ya, yb, x2, w_out, post_g, *, tm=1024):
    m = x2.shape[0]
    assert DA_WIDTH == ML_WIDTH
    row = lambda width: pl.BlockSpec((tm, width), lambda i: (i, 0))
    w_rows = lambda blk: pl.BlockSpec((DA_WIDTH, D_MODEL), lambda i: (blk, 0))
    return pl.pallas_call(
        _l0_out_proj_kernel,
        grid=(m // tm,),
        in_specs=[row(DA_WIDTH), row(ML_WIDTH), row(D_MODEL),
                  w_rows(0), w_rows(1), pl.BlockSpec((1, D_MODEL), lambda i: (0, 0))],
        out_specs=row(D_MODEL),
        out_shape=jax.ShapeDtypeStruct((m, D_MODEL), F32),
        compiler_params=pltpu.CompilerParams(
            dimension_semantics=("arbitrary",), vmem_limit_bytes=VMEM_LIMIT),
        name="l0_out_proj",
    )(ya, yb, x2, w_out, w_out, post_g)


def _l1_gmlp_kernel(h_ref, pre_g_ref, wu_ref, wv_ref, wz_ref, sg_g_ref, wsp_ref, bsp_ref,
                    wo_ref, post_g_ref, o_ref, vn_ref, y_ref, *, tm):
    hres = h_ref[...]
    hn = (hres * _rms_scale(hres) * pre_g_ref[...]).astype(BF16)

    v = _gelu_tanh(_dot(hn, wv_ref[...]))
    vn_ref[...] = (v * _rms_scale(v) * sg_g_ref[...]).astype(BF16)

    r_i = lax.broadcasted_iota(jnp.int32, (SG_CHUNK, SG_CHUNK), 0)
    c_i = lax.broadcasted_iota(jnp.int32, (SG_CHUNK, SG_CHUNK), 1)
    causal = c_i <= r_i
    for g in range(SG_GROUPS):
        cols = slice(g * SG_GROUP_DIM, (g + 1) * SG_GROUP_DIM)
        u = _gelu_tanh(_dot(hn, wu_ref[:, cols]))
        gate = _silu(_dot(hn, wz_ref[:, cols]))
        wm = jnp.where(causal, wsp_ref[g], 0.0).astype(BF16)
        bias = bsp_ref[g]
        for c in range(tm // SG_CHUNK):
            rows = slice(c * SG_CHUNK, (c + 1) * SG_CHUNK)
            vs = _dot(wm, vn_ref[rows, cols]) + bias
            y_ref[rows, cols] = (u[rows] * vs * gate[rows]).astype(BF16)

    out = _dot(y_ref[...], wo_ref[...])
    o_ref[...] = hres + out * _rms_scale(out) * post_g_ref[...]


def _l1_gmlp(h1, pre_g, w_in, sg_g, w_spatial, b_spatial, wo, post_g, *, tm=1024):
    m = h1.shape[0]
    row = pl.BlockSpec((tm, D_MODEL), lambda i: (i, 0))
    const2 = lambda shape: pl.BlockSpec(shape, lambda i: (0, 0), pipeline_mode=pl.Buffered(1))
    const3 = lambda shape: pl.BlockSpec(shape, lambda i: (0, 0, 0), pipeline_mode=pl.Buffered(1))
    w_cols = lambda blk: pl.BlockSpec((D_MODEL, SG_WIDTH), lambda i: (0, blk), pipeline_mode=pl.Buffered(1))
    return pl.pallas_call(
        functools.partial(_l1_gmlp_kernel, tm=tm),
        grid=(m // tm,),
        in_specs=[
            row, const2((1, D_MODEL)),
            w_cols(0), w_cols(1), w_cols(2),
            const2((1, SG_WIDTH)),
            const3((SG_GROUPS, SG_CHUNK, SG_CHUNK)), const3((SG_GROUPS, SG_CHUNK, 1)),
            const2((SG_WIDTH, D_MODEL)), const2((1, D_MODEL)),
        ],
        out_specs=row,
        out_shape=jax.ShapeDtypeStruct((m, D_MODEL), F32),
        scratch_shapes=[pltpu.VMEM((tm, SG_WIDTH), BF16), pltpu.VMEM((tm, SG_WIDTH), BF16)],
        compiler_params=pltpu.CompilerParams(
            dimension_semantics=("arbitrary",), vmem_limit_bytes=VMEM_LIMIT),
        name="l1_gmlp",
    )(h1, pre_g, w_in, w_in, w_in, sg_g, w_spatial, b_spatial, wo, post_g)


def kernel(x, l0_pre_g, l0_w_in, l0_b_igate, l0_b_fgate, l0_conv_w, l0_conv_b, l0_lambda_q1, l0_lambda_k1, l0_lambda_q2, l0_lambda_k2, l0_da_head_g, l0_ml_head_g, l0_w_out, l0_post_g, l1_pre_g, l1_w_in, l1_sg_norm_g, l1_w_spatial, l1_b_spatial, l1_w_out, l1_post_g):
    b, s, d = x.shape
    m = b * s
    x2 = x.reshape(m, d)
    row = lambda v: v.reshape(1, -1).astype(F32)

    w_in0 = l0_w_in.astype(BF16)
    w_main = jnp.concatenate([w_in0[:, :_OFF_BI], w_in0[:, _OFF_BO:]], axis=1)
    w_gate = jnp.pad(w_in0[:, _OFF_BI:_OFF_BO], ((0, 0), (0, LANES - 2 * ML_HEADS)))
    b_gate = jnp.pad(jnp.concatenate([l0_b_igate, l0_b_fgate]), (0, LANES - 2 * ML_HEADS)).reshape(1, LANES)
    lam_vecs = jnp.stack([l0_lambda_q1, l0_lambda_k1, l0_lambda_q2, l0_lambda_k2]).astype(F32)

    p, gates_t = _l0_in_proj(x2, row(l0_pre_g), w_main, w_gate, b_gate.astype(F32))
    p3 = p.reshape(b, s, L0_MAIN)
    gates_t = gates_t.reshape(2 * ML_HEADS, m // ML_CHUNK, ML_CHUNK)

    y_a = _diff_attn(p3, lam_vecs, row(l0_da_head_g))
    y_b = _mlstm(p3, gates_t, l0_conv_w.astype(F32), row(l0_conv_b), row(l0_ml_head_g))

    h1 = _l0_out_proj(y_a.reshape(m, DA_WIDTH), y_b.reshape(m, ML_WIDTH), x2,
                      l0_w_out.astype(BF16), row(l0_post_g))

    h2 = _l1_gmlp(h1, row(l1_pre_g), l1_w_in.astype(BF16),
                  row(l1_sg_norm_g), l1_w_spatial.astype(F32),
                  l1_b_spatial.astype(F32).reshape(SG_GROUPS, SG_CHUNK, 1),
                  l1_w_out.astype(BF16), row(l1_post_g))
    return h2.reshape(b, s, d)
```

```python
import functools
import math

import jax
import jax.numpy as jnp
from jax import lax
from jax.experimental import pallas as pl
from jax.experimental.pallas import tpu as pltpu

F32 = jnp.float32
BF16 = jnp.bfloat16

D_MODEL = 1024
EPS = 1e-6
DA_HEADS = 8
DA_QK_DIM = 64
DA_V_DIM = 128
DA_WIDTH = DA_HEADS * DA_V_DIM
ML_HEADS = 4
ML_QK_DIM = 128
ML_V_DIM = 256
ML_WIDTH = ML_HEADS * ML_V_DIM
ML_CHUNK = 128
CONV_WIDTH = 4
SG_GROUPS = 8
SG_CHUNK = 128
SG_WIDTH = 2 * D_MODEL
SG_GROUP_DIM = SG_WIDTH // SG_GROUPS

LANES = 128
SUM_ROWS = 16
VMEM_LIMIT = 56 * 1024 * 1024
NEG_BIG = -1e30
LOG2E = 1.4426950408889634

_OFF_AQ, _OFF_AK, _OFF_AV, _OFF_AZ = 0, 1024, 2048, 3072
_OFF_BQK, _OFF_BV, _OFF_BI, _OFF_BF, _OFF_BO, _OFF_BZ = 4096, 5120, 6144, 6148, 6152, 7176
L0_MAIN = 8192


def _sigmoid(x):
    return 1.0 / (1.0 + jnp.exp2(x * -LOG2E))


def _silu(x):
    return x * _sigmoid(x)


def _gelu_tanh(x):
    c = math.sqrt(2.0 / math.pi)
    return 0.5 * x * (1.0 + jnp.tanh(c * (x + 0.044715 * (x * x * x))))


def _rms_scale(x):
    return lax.rsqrt(jnp.mean(x * x, axis=-1, keepdims=True) + EPS)


def _dot(a, b):
    return jnp.dot(a, b, preferred_element_type=F32)


def _l0_in_proj_kernel(x_ref, g_ref, w_ref, cs_ref, wg_ref, bg_ref, p_ref, gt_ref, hn_ref):
    j = pl.program_id(1)

    @pl.when(j == 0)
    def _():
        x = x_ref[...]
        hn = (x * _rms_scale(x) * g_ref[...]).astype(BF16)
        hn_ref[...] = hn
        gates = _dot(hn, wg_ref[...]) + bg_ref[...]
        gt_ref[...] = gates.T[0:8, :]

    p_ref[...] = (_dot(hn_ref[...], w_ref[...]) * cs_ref[...]).astype(BF16)


def _l0_in_proj(x2, pre_g, w_main, w_gate, b_gate, *, tm=2048, tn=2048):
    m = x2.shape[0]
    q_scale = DA_QK_DIM ** -0.5 * LOG2E
    col_scale = jnp.where(jnp.arange(L0_MAIN) < DA_HEADS * 2 * DA_QK_DIM, q_scale, 1.0).astype(F32).reshape(1, -1)
    return pl.pallas_call(
        _l0_in_proj_kernel,
        grid=(m // tm, L0_MAIN // tn),
        in_specs=[
            pl.BlockSpec((tm, D_MODEL), lambda i, j: (i, 0)),
            pl.BlockSpec((1, D_MODEL), lambda i, j: (0, 0)),
            pl.BlockSpec((D_MODEL, tn), lambda i, j: (0, j)),
            pl.BlockSpec((1, tn), lambda i, j: (0, j)),
            pl.BlockSpec((D_MODEL, LANES), lambda i, j: (0, 0)),
            pl.BlockSpec((1, LANES), lambda i, j: (0, 0)),
        ],
        out_specs=[
            pl.BlockSpec((tm, tn), lambda i, j: (i, j)),
            pl.BlockSpec((8, tm), lambda i, j: (0, i)),
        ],
        out_shape=[
            jax.ShapeDtypeStruct((m, L0_MAIN), BF16),
            jax.ShapeDtypeStruct((8, m), F32),
        ],
        scratch_shapes=[pltpu.VMEM((tm, D_MODEL), BF16)],
        compiler_params=pltpu.CompilerParams(
            dimension_semantics=("arbitrary", "arbitrary"), vmem_limit_bytes=VMEM_LIMIT),
        name="l0_in_proj",
    )(x2, pre_g, w_main, col_scale, w_gate, b_gate)


def _diff_attn_kernel(vq_ref, vk_ref, lam_ref, q_ref, k_ref, v_ref, z_ref, g_ref, o_ref,
                      vt_ref, qbd_ref, acc_ref, m_ref, s0_ref, s1_ref, cmax0_ref, cmax1_ref, *, tq, lam_init):
    nq = vt_ref.shape[0]
    tk = tq
    n_off = vq_ref.shape[0]

    lam_v = lam_ref[...]
    lam = (jnp.exp(jnp.sum(lam_v[0:1] * lam_v[1:2], axis=-1, keepdims=True))
           - jnp.exp(jnp.sum(lam_v[2:3] * lam_v[3:4], axis=-1, keepdims=True)) + lam_init)

    half = tq // 2
    ones_row = (lax.broadcasted_iota(jnp.int32, (SUM_ROWS, tk), 0) == 0).astype(BF16)
    first = lax.broadcasted_iota(jnp.int32, (LANES, half), 0) < DA_QK_DIM
    zero = jnp.zeros((LANES, half), BF16)
    for i in range(nq):
        rows = slice(i * tq, (i + 1) * tq)
        vt_ref[i, 0:LANES, :] = v_ref[0, rows, :].T
        vt_ref[i, LANES:LANES + SUM_ROWS, :] = ones_row
        qt = q_ref[0, rows, :].T
        parts = []
        for qh in (qt[:, 0:half], qt[:, half:tq]):
            parts += [jnp.where(first, qh, zero), jnp.where(first, zero, qh)]
        qbd_ref[i] = jnp.concatenate(parts, axis=1)

    buf0, buf1 = (s0_ref, cmax0_ref), (s1_ref, cmax1_ref)

    def pipelined(n, issue, consume, issue_next_phase):
        def pair(i, issue_after):
            issue(buf1, i + 1)
            consume(buf0, i)
            issue_after()
            consume(buf1, i + 1)

        def body(t, carry):
            pair(2 * t, lambda: issue(buf0, 2 * t + 2))
            return carry

        lax.fori_loop(0, n // 2 - 1, body, 0, unroll=4)
        pair(n - 2, issue_next_phase)

    tri = (lax.broadcasted_iota(jnp.int32, (half, tq), 0)
           <= (lax.broadcasted_iota(jnp.int32, (half, tq), 1) & (half - 1)))

    def diag_scores(buf, i):
        s_ref, cmax_ref = buf
        r0 = pl.multiple_of(i * tk, tk)
        s_a = _dot(k_ref[0, pl.ds(r0, half), :], qbd_ref[i])
        s_b = _dot(k_ref[0, pl.ds(r0 + half, half), :], qbd_ref[i, :, tq:2 * tq])
        left = jnp.where(tri, s_a[:, 0:tq], NEG_BIG)
        right = s_a[:, tq:2 * tq]
        s_b = jnp.where(tri, s_b, NEG_BIG)
        s_ref[0:half, 0:tq] = left
        s_ref[0:half, tq:2 * tq] = right
        s_ref[half:tq, tq:2 * tq] = s_b
        cmax_ref[:, 0:tq] = jnp.max(left, axis=0, keepdims=True)
        cmax_ref[:, tq:2 * tq] = jnp.maximum(jnp.max(right, axis=0, keepdims=True),
                                             jnp.max(s_b, axis=0, keepdims=True))

    def diag_softmax_pv(buf, i):
        s_ref, cmax_ref = buf
        m_new = cmax_ref[...]
        p_a = jnp.exp2(s_ref[0:half, :] - m_new).astype(BF16)
        p_b = jnp.exp2(s_ref[half:tq, tq:2 * tq] - m_new[:, tq:2 * tq]).astype(BF16)
        pv_a = _dot(vt_ref[i, :, 0:half], p_a)
        pv_b = _dot(vt_ref[i, :, half:tq], p_b)
        acc_ref[i, :, 0:tq] = pv_a[:, 0:tq]
        acc_ref[i, :, tq:2 * tq] = pv_a[:, tq:2 * tq] + pv_b
        m_ref[i] = m_new

    def scores(buf, i):
        s_ref, cmax_ref = buf
        k = k_ref[0, pl.ds(pl.multiple_of(vk_ref[i] * tk, tk), tk), :]
        s = _dot(k, qbd_ref[vq_ref[i]])
        s_ref[...] = s
        cmax_ref[...] = jnp.max(s, axis=0, keepdims=True)

    def softmax_pv(buf, i):
        s_ref, cmax_ref = buf
        qi = vq_ref[i]
        m = m_ref[qi]
        m_new = jnp.maximum(m, cmax_ref[...])
        alpha = jnp.exp2(m - m_new)
        p = jnp.exp2(s_ref[...] - m_new).astype(BF16)
        acc_ref[qi] = alpha * acc_ref[qi] + _dot(vt_ref[vk_ref[i]], p)
        m_ref[qi] = m_new

    diag_scores(buf0, 0)
    pipelined(nq, diag_scores, diag_softmax_pv, lambda: scores(buf0, 0))
    pipelined(n_off, scores, softmax_pv, lambda: None)

    gain_col = jnp.broadcast_to(g_ref[...] * (1.0 - lam_init), (8, LANES)).T[:, 0:1]
    for i in range(nq):
        acc = acc_ref[i]
        o_t = acc[0:LANES, :] / acc[LANES:LANES + 1, :]
        o1 = jnp.concatenate([o_t[:, 0:half], o_t[:, tq:tq + half]], axis=1)
        o2 = jnp.concatenate([o_t[:, half:tq], o_t[:, tq + half:2 * tq]], axis=1)
        d_t = o1 - lam * o2
        rinv = lax.rsqrt(jnp.mean(d_t * d_t, axis=0, keepdims=True) + EPS)
        o = (d_t * rinv * gain_col).T
        rows = slice(i * tq, (i + 1) * tq)
        o_ref[0, rows, :] = (o * _silu(z_ref[0, rows, :].astype(F32))).astype(BF16)


def _diff_attn(p3, lam_vecs, head_g, *, tq=512):
    b, s, _ = p3.shape
    nq = s // tq
    hq, hk, hv, hz = (off // LANES for off in (_OFF_AQ, _OFF_AK, _OFF_AV, _OFF_AZ))
    layer = 0
    lam_init = 0.8 - 0.6 * math.exp(-0.3 * layer)
    below = [(qi, kb) for qi in range(nq) for kb in range(qi)]
    assert nq % 2 == 0 and len(below) % 2 == 0
    visit_q = jnp.asarray([qi for qi, _ in below], jnp.int32)
    visit_k = jnp.asarray([kb for _, kb in below], jnp.int32)
    seq = lambda col0: pl.BlockSpec((1, s, LANES), lambda bi, h, vq, vk: (bi, 0, col0 + h))
    return pl.pallas_call(
        functools.partial(_diff_attn_kernel, tq=tq, lam_init=lam_init),
        grid_spec=pltpu.PrefetchScalarGridSpec(
            num_scalar_prefetch=2,
            grid=(b, DA_HEADS),
            in_specs=[
                pl.BlockSpec((4, DA_QK_DIM), lambda bi, h, vq, vk: (0, 0)),
                seq(hq), seq(hk), seq(hv), seq(hz),
                pl.BlockSpec((1, DA_V_DIM), lambda bi, h, vq, vk: (0, 0)),
            ],
            out_specs=pl.BlockSpec((1, s, LANES), lambda bi, h, vq, vk: (bi, 0, h)),
            scratch_shapes=[
                pltpu.VMEM((nq, LANES + SUM_ROWS, tq), BF16),
                pltpu.VMEM((nq, LANES, 2 * tq), BF16),
                pltpu.VMEM((nq, LANES + SUM_ROWS, 2 * tq), F32),
                pltpu.VMEM((nq, 1, 2 * tq), F32),
                pltpu.VMEM((tq, 2 * tq), F32),
                pltpu.VMEM((tq, 2 * tq), F32),
                pltpu.VMEM((1, 2 * tq), F32),
                pltpu.VMEM((1, 2 * tq), F32),
            ],
        ),
        out_shape=jax.ShapeDtypeStruct((b, s, DA_WIDTH), BF16),
        compiler_params=pltpu.CompilerParams(
            dimension_semantics=("arbitrary", "arbitrary"), vmem_limit_bytes=VMEM_LIMIT),
        name="diff_attn",
    )(visit_q, visit_k, lam_vecs, p3, p3, p3, p3, head_g)


def _mlstm_kernel(qp_ref, kp_ref, v_ref, o_ref, z_ref, gt_ref, cwq_ref, cwk_ref, cbq_ref, cbk_ref,
                  hg_ref, y_ref, xq_ref, xk_ref, ic_ref, bc_ref, c_ref):
    h = pl.program_id(1)
    s = qp_ref.shape[1]
    L = ML_CHUNK
    nc = s // L
    pad = 8

    zeros = jnp.zeros((pad, ML_QK_DIM), F32)
    xq_ref[0:pad, :] = zeros
    xk_ref[0:pad, :] = zeros
    xq_ref[pad:, :] = qp_ref[0].astype(F32)
    xk_ref[pad:, :] = kp_ref[0].astype(F32)

    ic = gt_ref[h]
    fp = gt_ref[ML_HEADS + h]
    lf = jnp.minimum(fp, 0.0) - jnp.log(1.0 + jnp.exp(-jnp.abs(fp)))
    r_i = lax.broadcasted_iota(jnp.int32, (L, L), 0)
    c_i = lax.broadcasted_iota(jnp.int32, (L, L), 1)
    tri_u = (r_i <= c_i).astype(BF16)
    lf_hi = lf.astype(BF16)
    lf_lo = (lf - lf_hi.astype(F32)).astype(BF16)
    ic_ref[...] = ic
    bc_ref[...] = _dot(lf_hi, tri_u) + _dot(lf_lo, tri_u)

    c_ref[...] = jnp.zeros_like(c_ref)
    causal = c_i <= r_i
    ones_col = (lax.broadcasted_iota(jnp.int32, (L, LANES), 1) == 0).astype(BF16)
    cwq = cwq_ref[...]
    cwk = cwk_ref[...]

    def conv(x_ref, r0, w, bias):
        xw = x_ref[pl.ds(r0, L + pad), :]
        acc = bias
        for j in range(CONV_WIDTH):
            lo = pad - (CONV_WIDTH - 1) + j
            acc = acc + w[j:j + 1, :] * xw[lo:lo + L, :]
        return _silu(acc)

    def chunk(c, m_prev):
        r0 = pl.multiple_of(c * L, L)
        q = (conv(xq_ref, r0, cwq, cbq_ref[...]) * ML_QK_DIM ** -0.5).astype(BF16)
        k_t = conv(xk_ref, r0, cwk, cbk_ref[...]).T
        v_ext = jnp.concatenate([v_ref[0, pl.ds(r0, L), :], ones_col], axis=1)

        ic_row = ic_ref[pl.ds(c, 1), :]
        bc_row = bc_ref[pl.ds(c, 1), :]
        bc_col = jnp.broadcast_to(bc_row, (8, L)).T[:, 0:1]
        b_last = bc_row[:, L - 1:L]

        log_d = jnp.where(causal, bc_col - bc_row + ic_row, NEG_BIG)
        log_inter = bc_col + m_prev
        m_t = jnp.maximum(log_inter, jnp.max(log_d, axis=-1, keepdims=True))
        dmat = jnp.exp(log_d - m_t)
        inter_w = jnp.exp(log_inter - m_t)
        qk = (_dot(q, k_t.astype(BF16)) * dmat).astype(BF16)
        nd = _dot(qk, v_ext) + inter_w * _dot(q, c_ref[...].astype(BF16))
        num = nd[:, 0:ML_V_DIM]
        den = nd[:, ML_V_DIM:ML_V_DIM + 1]
        hm = num / jnp.maximum(jnp.abs(den), jnp.exp(-m_t))
        hm = hm * _sigmoid(o_ref[0, pl.ds(r0, L), :].astype(F32))
        hm = hm * _rms_scale(hm) * hg_ref[...]
        y_ref[0, pl.ds(r0, L), :] = (hm * _silu(z_ref[0, pl.ds(r0, L), :].astype(F32))).astype(BF16)

        g_row = b_last - bc_row + ic_row
        m_new = jnp.maximum(b_last + m_prev, jnp.max(g_row, axis=-1, keepdims=True))
        decay = jnp.exp(b_last + m_prev - m_new)
        w_row = jnp.exp(g_row - m_new)
        c_ref[...] = decay * c_ref[...] + _dot((k_t * w_row).astype(BF16), v_ext)
        return m_new

    lax.fori_loop(0, nc, chunk, jnp.zeros((1, 1), F32), unroll=8)


def _mlstm(p3, gates_t, conv_w, conv_b, head_g):
    b, s, _ = p3.shape
    nc = s // ML_CHUNK
    cq = _OFF_BQK // ML_QK_DIM
    ck = cq + ML_HEADS
    n_gate = _OFF_BO - _OFF_BI
    cv, co, cz = (off // ML_V_DIM for off in (_OFF_BV, _OFF_BO - n_gate, _OFF_BZ - n_gate))
    seq = lambda width, col0: pl.BlockSpec((1, s, width), lambda bi, h: (bi, 0, col0 + h))
    return pl.pallas_call(
        _mlstm_kernel,
        grid=(b, ML_HEADS),
        in_specs=[
            seq(ML_QK_DIM, cq), seq(ML_QK_DIM, ck), seq(ML_V_DIM, cv), seq(ML_V_DIM, co), seq(ML_V_DIM, cz),
            pl.BlockSpec((2 * ML_HEADS, nc, ML_CHUNK), lambda bi, h: (0, bi, 0)),
            pl.BlockSpec((CONV_WIDTH, ML_QK_DIM), lambda bi, h: (0, h)),
            pl.BlockSpec((CONV_WIDTH, ML_QK_DIM), lambda bi, h: (0, ML_HEADS + h)),
            pl.BlockSpec((1, ML_QK_DIM), lambda bi, h: (0, h)),
            pl.BlockSpec((1, ML_QK_DIM), lambda bi, h: (0, ML_HEADS + h)),
            pl.BlockSpec((1, ML_V_DIM), lambda bi, h: (0, 0)),
        ],
        out_specs=pl.BlockSpec((1, s, ML_V_DIM), lambda bi, h: (bi, 0, h)),
        out_shape=jax.ShapeDtypeStruct((b, s, ML_WIDTH), BF16),
        scratch_shapes=[
            pltpu.VMEM((s + 8, ML_QK_DIM), F32),
            pltpu.VMEM((s + 8, ML_QK_DIM), F32),
            pltpu.VMEM((nc, ML_CHUNK), F32),
            pltpu.VMEM((nc, ML_CHUNK), F32),
            pltpu.VMEM((ML_QK_DIM, ML_V_DIM + LANES), F32),
        ],
        compiler_params=pltpu.CompilerParams(
            dimension_semantics=("arbitrary", "arbitrary"), vmem_limit_bytes=VMEM_LIMIT),
        name="mlstm",
    )(p3, p3, p3, p3, p3, gates_t, conv_w, conv_w, conv_b, conv_b, head_g)


def _tail_kernel(ya_ref, yb_ref, x_ref, wa_ref, wb_ref, post0_g_ref, pre_g_ref, wu_ref, wv_ref, wz_ref,
                 sg_g_ref, wsp_ref, bsp_ref, wo_ref, post_g_ref, o_ref, vn_ref, y_ref, *, tm):
    y0 = _dot(ya_ref[...], wa_ref[...]) + _dot(yb_ref[...], wb_ref[...])
    o_ref[...] = x_ref[...] + y0 * _rms_scale(y0) * post0_g_ref[...]
    hres = o_ref[...]
    hn = (hres * _rms_scale(hres) * pre_g_ref[...]).astype(BF16)

    v = _gelu_tanh(_dot(hn, wv_ref[...]))
    vn_ref[...] = (v * _rms_scale(v) * sg_g_ref[...]).astype(BF16)

    r_i = lax.broadcasted_iota(jnp.int32, (SG_CHUNK, SG_CHUNK), 0)
    c_i = lax.broadcasted_iota(jnp.int32, (SG_CHUNK, SG_CHUNK), 1)
    causal = c_i <= r_i
    for g in range(SG_GROUPS):
        cols = slice(g * SG_GROUP_DIM, (g + 1) * SG_GROUP_DIM)
        u = _gelu_tanh(_dot(hn, wu_ref[:, cols]))
        gate = _silu(_dot(hn, wz_ref[:, cols]))
        wm = jnp.where(causal, wsp_ref[g], 0.0).astype(BF16)
        bias = bsp_ref[g]
        for c in range(tm // SG_CHUNK):
            rows = slice(c * SG_CHUNK, (c + 1) * SG_CHUNK)
            vs = _dot(wm, vn_ref[rows, cols]) + bias
            y_ref[rows, cols] = (u[rows] * vs * gate[rows]).astype(BF16)

    out = _dot(y_ref[...], wo_ref[...])
    o_ref[...] = o_ref[...] + out * _rms_scale(out) * post_g_ref[...]


def _tail(ya, yb, x2, w_out0, post0_g, pre_g, w_in, sg_g, w_spatial, b_spatial, wo, post_g, *, tm=512):
    m = x2.shape[0]
    assert DA_WIDTH == ML_WIDTH
    row = lambda width: pl.BlockSpec((tm, width), lambda i: (i, 0))
    once = dict(pipeline_mode=pl.Buffered(1))
    const2 = lambda shape: pl.BlockSpec(shape, lambda i: (0, 0), **once)
    const3 = lambda shape: pl.BlockSpec(shape, lambda i: (0, 0, 0), **once)
    w_rows = lambda blk: pl.BlockSpec((DA_WIDTH, D_MODEL), lambda i: (blk, 0), **once)
    w_cols = lambda blk: pl.BlockSpec((D_MODEL, SG_WIDTH), lambda i: (0, blk), **once)
    return pl.pallas_call(
        functools.partial(_tail_kernel, tm=tm),
        grid=(m // tm,),
        in_specs=[
            row(DA_WIDTH), row(ML_WIDTH), row(D_MODEL),
            w_rows(0), w_rows(1), const2((1, D_MODEL)), const2((1, D_MODEL)),
            w_cols(0), w_cols(1), w_cols(2),
            const2((1, SG_WIDTH)),
            const3((SG_GROUPS, SG_CHUNK, SG_CHUNK)), const3((SG_GROUPS, SG_CHUNK, 1)),
            const2((SG_WIDTH, D_MODEL)), const2((1, D_MODEL)),
        ],
        out_specs=row(D_MODEL),
        out_shape=jax.ShapeDtypeStruct((m, D_MODEL), F32),
        scratch_shapes=[pltpu.VMEM((tm, SG_WIDTH), BF16), pltpu.VMEM((tm, SG_WIDTH), BF16)],
        compiler_params=pltpu.CompilerParams(
            dimension_semantics=("arbitrary",), vmem_limit_bytes=VMEM_LIMIT),
        name="l0_out_l1_gmlp",
    )(ya, yb, x2, w_out0, w_out0, post0_g, pre_g, w_in, w_in, w_in, sg_g, w_spatial, b_spatial, wo, post_g)


def kernel(x, l0_pre_g, l0_w_in, l0_b_igate, l0_b_fgate, l0_conv_w, l0_conv_b, l0_lambda_q1, l0_lambda_k1, l0_lambda_q2, l0_lambda_k2, l0_da_head_g, l0_ml_head_g, l0_w_out, l0_post_g, l1_pre_g, l1_w_in, l1_sg_norm_g, l1_w_spatial, l1_b_spatial, l1_w_out, l1_post_g):
    b, s, d = x.shape
    m = b * s
    x2 = x.reshape(m, d)
    row = lambda v: v.reshape(1, -1).astype(F32)

    w_in0 = l0_w_in.astype(BF16)
    w_main = jnp.concatenate([w_in0[:, :_OFF_BI], w_in0[:, _OFF_BO:]], axis=1)
    w_gate = jnp.pad(w_in0[:, _OFF_BI:_OFF_BO], ((0, 0), (0, LANES - 2 * ML_HEADS)))
    b_gate = jnp.pad(jnp.concatenate([l0_b_igate, l0_b_fgate]), (0, LANES - 2 * ML_HEADS)).reshape(1, LANES)
    lam_vecs = jnp.stack([l0_lambda_q1, l0_lambda_k1, l0_lambda_q2, l0_lambda_k2]).astype(F32)

    p, gates_t = _l0_in_proj(x2, row(l0_pre_g), w_main, w_gate, b_gate.astype(F32))
    p3 = p.reshape(b, s, L0_MAIN)
    gates_t = gates_t.reshape(2 * ML_HEADS, m // ML_CHUNK, ML_CHUNK)

    y_a = _diff_attn(p3, lam_vecs, row(l0_da_head_g))
    y_b = _mlstm(p3, gates_t, l0_conv_w.astype(F32), row(l0_conv_b), row(l0_ml_head_g))

    h2 = _tail(y_a.reshape(m, DA_WIDTH), y_b.reshape(m, ML_WIDTH), x2,
               l0_w_out.astype(BF16), row(l0_post_g), row(l1_pre_g), l1_w_in.astype(BF16),
               row(l1_sg_norm_g), l1_w_spatial.astype(F32),
               l1_b_spatial.astype(F32).reshape(SG_GROUPS, SG_CHUNK, 1),
               l1_w_out.astype(BF16), row(l1_post_g))
    return h2.reshape(b, s, d)
```

```python
import functools
import math

import jax
import jax.numpy as jnp
from jax import lax
from jax.experimental import pallas as pl
from jax.experimental.pallas import tpu as pltpu

F32 = jnp.float32
BF16 = jnp.bfloat16

D_MODEL = 1024
EPS = 1e-6
DA_HEADS = 8
DA_QK_DIM = 64
DA_V_DIM = 128
DA_WIDTH = DA_HEADS * DA_V_DIM
ML_HEADS = 4
ML_QK_DIM = 128
ML_V_DIM = 256
ML_WIDTH = ML_HEADS * ML_V_DIM
ML_CHUNK = 128
CONV_WIDTH = 4
SG_GROUPS = 8
SG_CHUNK = 128
SG_WIDTH = 2 * D_MODEL
SG_GROUP_DIM = SG_WIDTH // SG_GROUPS

LANES = 128
SUM_ROWS = 16
VMEM_LIMIT = 56 * 1024 * 1024
NEG_BIG = -1e30
LOG2E = 1.4426950408889634

_OFF_AQ, _OFF_AK, _OFF_AV, _OFF_AZ = 0, 1024, 2048, 3072
_OFF_BQK, _OFF_BV, _OFF_BI, _OFF_BF, _OFF_BO, _OFF_BZ = 4096, 5120, 6144, 6148, 6152, 7176
L0_MAIN = 8192


def _sigmoid(x):
    return 1.0 / (1.0 + jnp.exp2(x * -LOG2E))


def _silu(x):
    return x * _sigmoid(x)


def _gelu_tanh(x):
    c = math.sqrt(2.0 / math.pi)
    return 0.5 * x * (1.0 + jnp.tanh(c * (x + 0.044715 * (x * x * x))))


def _rms_scale(x):
    return lax.rsqrt(jnp.mean(x * x, axis=-1, keepdims=True) + EPS)


def _dot(a, b):
    return jnp.dot(a, b, preferred_element_type=F32)


def _l0_in_proj_kernel(x_ref, g_ref, w_ref, cs_ref, wg_ref, bg_ref, p_ref, gt_ref, hn_ref):
    j = pl.program_id(1)

    @pl.when(j == 0)
    def _():
        x = x_ref[...]
        hn = (x * _rms_scale(x) * g_ref[...]).astype(BF16)
        hn_ref[...] = hn
        gates = _dot(hn, wg_ref[...]) + bg_ref[...]
        gt_ref[...] = gates.T[0:8, :]

    p_ref[...] = (_dot(hn_ref[...], w_ref[...]) * cs_ref[...]).astype(BF16)


def _l0_in_proj(x2, pre_g, w_main, w_gate, b_gate, *, tm=2048, tn=2048):
    m = x2.shape[0]
    q_scale = DA_QK_DIM ** -0.5 * LOG2E
    col_scale = jnp.where(jnp.arange(L0_MAIN) < DA_HEADS * 2 * DA_QK_DIM, q_scale, 1.0).astype(F32).reshape(1, -1)
    return pl.pallas_call(
        _l0_in_proj_kernel,
        grid=(m // tm, L0_MAIN // tn),
        in_specs=[
            pl.BlockSpec((tm, D_MODEL), lambda i, j: (i, 0)),
            pl.BlockSpec((1, D_MODEL), lambda i, j: (0, 0)),
            pl.BlockSpec((D_MODEL, tn), lambda i, j: (0, j)),
            pl.BlockSpec((1, tn), lambda i, j: (0, j)),
            pl.BlockSpec((D_MODEL, LANES), lambda i, j: (0, 0)),
            pl.BlockSpec((1, LANES), lambda i, j: (0, 0)),
        ],
        out_specs=[
            pl.BlockSpec((tm, tn), lambda i, j: (i, j)),
            pl.BlockSpec((8, tm), lambda i, j: (0, i)),
        ],
        out_shape=[
            jax.ShapeDtypeStruct((m, L0_MAIN), BF16),
            jax.ShapeDtypeStruct((8, m), F32),
        ],
        scratch_shapes=[pltpu.VMEM((tm, D_MODEL), BF16)],
        compiler_params=pltpu.CompilerParams(
            dimension_semantics=("arbitrary", "arbitrary"), vmem_limit_bytes=VMEM_LIMIT),
        name="l0_in_proj",
    )(x2, pre_g, w_main, col_scale, w_gate, b_gate)


def _diff_attn_kernel(vq_ref, vk_ref, lam_ref, q_ref, k_ref, v_ref, z_ref, g_ref, o_ref,
                      vt_ref, qbd_ref, acc_ref, m_ref, s0_ref, s1_ref, cmax0_ref, cmax1_ref, *, tq, lam_init):
    nq = vt_ref.shape[0]
    tk = tq
    n_off = vq_ref.shape[0]

    lam_v = lam_ref[...]
    lam = (jnp.exp(jnp.sum(lam_v[0:1] * lam_v[1:2], axis=-1, keepdims=True))
           - jnp.exp(jnp.sum(lam_v[2:3] * lam_v[3:4], axis=-1, keepdims=True)) + lam_init)

    half = tq // 2
    ones_row = (lax.broadcasted_iota(jnp.int32, (SUM_ROWS, tk), 0) == 0).astype(BF16)
    first = lax.broadcasted_iota(jnp.int32, (LANES, half), 0) < DA_QK_DIM
    zero = jnp.zeros((LANES, half), BF16)
    for i in range(nq):
        rows = slice(i * tq, (i + 1) * tq)
        vt_ref[i, 0:LANES, :] = v_ref[0, rows, :].T
        vt_ref[i, LANES:LANES + SUM_ROWS, :] = ones_row
        qt = q_ref[0, rows, :].T
        parts = []
        for qh in (qt[:, 0:half], qt[:, half:tq]):
            parts += [jnp.where(first, qh, zero), jnp.where(first, zero, qh)]
        qbd_ref[i] = jnp.concatenate(parts, axis=1)

    buf0, buf1 = (s0_ref, cmax0_ref), (s1_ref, cmax1_ref)

    def pipelined(n, issue, consume, issue_next_phase):
        def pair(i, issue_after):
            issue(buf1, i + 1)
            consume(buf0, i)
            issue_after()
            consume(buf1, i + 1)

        def body(t, carry):
            pair(2 * t, lambda: issue(buf0, 2 * t + 2))
            return carry

        lax.fori_loop(0, n // 2 - 1, body, 0, unroll=4)
        pair(n - 2, issue_next_phase)

    tri = (lax.broadcasted_iota(jnp.int32, (half, tq), 0)
           <= (lax.broadcasted_iota(jnp.int32, (half, tq), 1) & (half - 1)))

    def diag_scores(buf, i):
        s_ref, cmax_ref = buf
        r0 = pl.multiple_of(i * tk, tk)
        s_a = _dot(k_ref[0, pl.ds(r0, half), :], qbd_ref[i])
        s_b = _dot(k_ref[0, pl.ds(r0 + half, half), :], qbd_ref[i, :, tq:2 * tq])
        left = jnp.where(tri, s_a[:, 0:tq], NEG_BIG)
        right = s_a[:, tq:2 * tq]
        s_b = jnp.where(tri, s_b, NEG_BIG)
        s_ref[0:half, 0:tq] = left
        s_ref[0:half, tq:2 * tq] = right
        s_ref[half:tq, tq:2 * tq] = s_b
        cmax_ref[:, 0:tq] = jnp.max(left, axis=0, keepdims=True)
        cmax_ref[:, tq:2 * tq] = jnp.maximum(jnp.max(right, axis=0, keepdims=True),
                                             jnp.max(s_b, axis=0, keepdims=True))

    def diag_softmax_pv(buf, i):
        s_ref, cmax_ref = buf
        m_new = cmax_ref[...]
        p_a = jnp.exp2(s_ref[0:half, :] - m_new).astype(BF16)
        p_b = jnp.exp2(s_ref[half:tq, tq:2 * tq] - m_new[:, tq:2 * tq]).astype(BF16)
        pv_a = _dot(vt_ref[i, :, 0:half], p_a)
        pv_b = _dot(vt_ref[i, :, half:tq], p_b)
        acc_ref[i, :, 0:tq] = pv_a[:, 0:tq]
        acc_ref[i, :, tq:2 * tq] = pv_a[:, tq:2 * tq] + pv_b
        m_ref[i] = m_new

    def scores(buf, i):
        s_ref, cmax_ref = buf
        k = k_ref[0, pl.ds(pl.multiple_of(vk_ref[i] * tk, tk), tk), :]
        s = _dot(k, qbd_ref[vq_ref[i]])
        s_ref[...] = s
        cmax_ref[...] = jnp.max(s, axis=0, keepdims=True)

    def softmax_pv(buf, i):
        s_ref, cmax_ref = buf
        qi = vq_ref[i]
        m = m_ref[qi]
        m_new = jnp.maximum(m, cmax_ref[...])
        alpha = jnp.exp2(m - m_new)
        p = jnp.exp2(s_ref[...] - m_new).astype(BF16)
        acc_ref[qi] = alpha * acc_ref[qi] + _dot(vt_ref[vk_ref[i]], p)
        m_ref[qi] = m_new

    diag_scores(buf0, 0)
    pipelined(nq, diag_scores, diag_softmax_pv, lambda: scores(buf0, 0))
    pipelined(n_off, scores, softmax_pv, lambda: None)

    gain_col = jnp.broadcast_to(g_ref[...] * (1.0 - lam_init), (8, LANES)).T[:, 0:1]
    for i in range(nq):
        acc = acc_ref[i]
        o_t = acc[0:LANES, :] / acc[LANES:LANES + 1, :]
        o1 = jnp.concatenate([o_t[:, 0:half], o_t[:, tq:tq + half]], axis=1)
        o2 = jnp.concatenate([o_t[:, half:tq], o_t[:, tq + half:2 * tq]], axis=1)
        d_t = o1 - lam * o2
        rinv = lax.rsqrt(jnp.mean(d_t * d_t, axis=0, keepdims=True) + EPS)
        o = (d_t * rinv * gain_col).T
        rows = slice(i * tq, (i + 1) * tq)
        o_ref[0, rows, :] = (o * _silu(z_ref[0, rows, :].astype(F32))).astype(BF16)


def _diff_attn(p3, lam_vecs, head_g, *, tq=512):
    b, s, _ = p3.shape
    nq = s // tq
    hq, hk, hv, hz = (off // LANES for off in (_OFF_AQ, _OFF_AK, _OFF_AV, _OFF_AZ))
    layer = 0
    lam_init = 0.8 - 0.6 * math.exp(-0.3 * layer)
    below = [(qi, kb) for qi in range(nq) for kb in range(qi)]
    assert nq % 2 == 0 and len(below) % 2 == 0
    visit_q = jnp.asarray([qi for qi, _ in below], jnp.int32)
    visit_k = jnp.asarray([kb for _, kb in below], jnp.int32)
    seq = lambda col0: pl.BlockSpec((1, s, LANES), lambda bi, h, vq, vk: (bi, 0, col0 + h))
    return pl.pallas_call(
        functools.partial(_diff_attn_kernel, tq=tq, lam_init=lam_init),
        grid_spec=pltpu.PrefetchScalarGridSpec(
            num_scalar_prefetch=2,
            grid=(b, DA_HEADS),
            in_specs=[
                pl.BlockSpec((4, DA_QK_DIM), lambda bi, h, vq, vk: (0, 0)),
                seq(hq), seq(hk), seq(hv), seq(hz),
                pl.BlockSpec((1, DA_V_DIM), lambda bi, h, vq, vk: (0, 0)),
            ],
            out_specs=pl.BlockSpec((1, s, LANES), lambda bi, h, vq, vk: (bi, 0, h)),
            scratch_shapes=[
                pltpu.VMEM((nq, LANES + SUM_ROWS, tq), BF16),
                pltpu.VMEM((nq, LANES, 2 * tq), BF16),
                pltpu.VMEM((nq, LANES + SUM_ROWS, 2 * tq), F32),
                pltpu.VMEM((nq, 1, 2 * tq), F32),
                pltpu.VMEM((tq, 2 * tq), F32),
                pltpu.VMEM((tq, 2 * tq), F32),
                pltpu.VMEM((1, 2 * tq), F32),
                pltpu.VMEM((1, 2 * tq), F32),
            ],
        ),
        out_shape=jax.ShapeDtypeStruct((b, s, DA_WIDTH), BF16),
        compiler_params=pltpu.CompilerParams(
            dimension_semantics=("arbitrary", "arbitrary"), vmem_limit_bytes=VMEM_LIMIT),
        name="diff_attn",
    )(visit_q, visit_k, lam_vecs, p3, p3, p3, p3, head_g)


def _mlstm_kernel(qp_ref, kp_ref, v_ref, o_ref, z_ref, gt_ref, cwq_ref, cwk_ref, cbq_ref, cbk_ref,
                  hg_ref, y_ref, xq_ref, xk_ref, ic_ref, bc_ref, c_ref):
    h = pl.program_id(1)
    s = qp_ref.shape[1]
    L = ML_CHUNK
    nc = s // L
    pad = 8

    zeros = jnp.zeros((pad, ML_QK_DIM), F32)
    xq_ref[0:pad, :] = zeros
    xk_ref[0:pad, :] = zeros
    xq_ref[pad:, :] = qp_ref[0].astype(F32)
    xk_ref[pad:, :] = kp_ref[0].astype(F32)

    ic = gt_ref[h]
    fp = gt_ref[ML_HEADS + h]
    lf = jnp.minimum(fp, 0.0) - jnp.log(1.0 + jnp.exp(-jnp.abs(fp)))
    r_i = lax.broadcasted_iota(jnp.int32, (L, L), 0)
    c_i = lax.broadcasted_iota(jnp.int32, (L, L), 1)
    tri_u = (r_i <= c_i).astype(BF16)
    lf_hi = lf.astype(BF16)
    lf_lo = (lf - lf_hi.astype(F32)).astype(BF16)
    ic_ref[...] = ic
    bc_ref[...] = _dot(lf_hi, tri_u) + _dot(lf_lo, tri_u)

    c_ref[...] = jnp.zeros_like(c_ref)
    causal = c_i <= r_i
    ones_col = (lax.broadcasted_iota(jnp.int32, (L, LANES), 1) == 0).astype(BF16)

    def conv(x_ref, r0, w, bias):
        xw = x_ref[pl.ds(r0, L + pad), :]
        acc = bias
        for j in range(CONV_WIDTH):
            lo = pad - (CONV_WIDTH - 1) + j
            acc = acc + w[j:j + 1, :] * xw[lo:lo + L, :]
        return _silu(acc)

    def chunk(c, m_prev):
        r0 = pl.multiple_of(c * L, L)
        q = (conv(xq_ref, r0, cwq_ref[...], cbq_ref[...]) * ML_QK_DIM ** -0.5).astype(BF16)
        k_t = conv(xk_ref, r0, cwk_ref[...], cbk_ref[...]).T
        v_ext = jnp.concatenate([v_ref[0, pl.ds(r0, L), :], ones_col], axis=1)

        ic_row = ic_ref[pl.ds(c, 1), :]
        bc_row = bc_ref[pl.ds(c, 1), :]
        bc_col = jnp.broadcast_to(bc_row, (8, L)).T[:, 0:1]
        b_last = bc_row[:, L - 1:L]

        log_d = jnp.where(causal, bc_col - bc_row + ic_row, NEG_BIG)
        log_inter = bc_col + m_prev
        m_t = jnp.maximum(log_inter, jnp.max(log_d, axis=-1, keepdims=True))
        dmat = jnp.exp(log_d - m_t)
        inter_w = jnp.exp(log_inter - m_t)
        qk = (_dot(q, k_t.astype(BF16)) * dmat).astype(BF16)
        nd = _dot(qk, v_ext) + inter_w * _dot(q, c_ref[...].astype(BF16))
        num = nd[:, 0:ML_V_DIM]
        den = nd[:, ML_V_DIM:ML_V_DIM + 1]
        hm = num / jnp.maximum(jnp.abs(den), jnp.exp(-m_t))
        hm = hm * _sigmoid(o_ref[0, pl.ds(r0, L), :].astype(F32))
        hm = hm * _rms_scale(hm) * hg_ref[...]
        y_ref[0, pl.ds(r0, L), :] = (hm * _silu(z_ref[0, pl.ds(r0, L), :].astype(F32))).astype(BF16)

        g_row = b_last - bc_row + ic_row
        m_new = jnp.maximum(b_last + m_prev, jnp.max(g_row, axis=-1, keepdims=True))
        decay = jnp.exp(b_last + m_prev - m_new)
        w_row = jnp.exp(g_row - m_new)
        c_ref[...] = decay * c_ref[...] + _dot((k_t * w_row).astype(BF16), v_ext)
        return m_new

    lax.fori_loop(0, nc, chunk, jnp.zeros((1, 1), F32), unroll=8)


def _mlstm(p3, gates_t, conv_w, conv_b, head_g):
    b, s, _ = p3.shape
    nc = s // ML_CHUNK
    cq = _OFF_BQK // ML_QK_DIM
    ck = cq + ML_HEADS
    n_gate = _OFF_BO - _OFF_BI
    cv, co, cz = (off // ML_V_DIM for off in (_OFF_BV, _OFF_BO - n_gate, _OFF_BZ - n_gate))
    seq = lambda width, col0: pl.BlockSpec((1, s, width), lambda bi, h: (bi, 0, col0 + h))
    return pl.pallas_call(
        _mlstm_kernel,
        grid=(b, ML_HEADS),
        in_specs=[
            seq(ML_QK_DIM, cq), seq(ML_QK_DIM, ck), seq(ML_V_DIM, cv), seq(ML_V_DIM, co), seq(ML_V_DIM, cz),
            pl.BlockSpec((2 * ML_HEADS, nc, ML_CHUNK), lambda bi, h: (0, bi, 0)),
            pl.BlockSpec((CONV_WIDTH, ML_QK_DIM), lambda bi, h: (0, h)),
            pl.BlockSpec((CONV_WIDTH, ML_QK_DIM), lambda bi, h: (0, ML_HEADS + h)),
            pl.BlockSpec((1, ML_QK_DIM), lambda bi, h: (0, h)),
            pl.BlockSpec((1, ML_QK_DIM), lambda bi, h: (0, ML_HEADS + h)),
            pl.BlockSpec((1, ML_V_DIM), lambda bi, h: (0, 0)),
        ],
        out_specs=pl.BlockSpec((1, s, ML_V_DIM), lambda bi, h: (bi, 0, h)),
        out_shape=jax.ShapeDtypeStruct((b, s, ML_WIDTH), BF16),
        scratch_shapes=[
            pltpu.VMEM((s + 8, ML_QK_DIM), F32),
            pltpu.VMEM((s + 8, ML_QK_DIM), F32),
            pltpu.VMEM((nc, ML_CHUNK), F32),
            pltpu.VMEM((nc, ML_CHUNK), F32),
            pltpu.VMEM((ML_QK_DIM, ML_V_DIM + LANES), F32),
        ],
        compiler_params=pltpu.CompilerParams(
            dimension_semantics=("arbitrary", "arbitrary"), vmem_limit_bytes=VMEM_LIMIT),
        name="mlstm",
    )(p3, p3, p3, p3, p3, gates_t, conv_w, conv_w, conv_b, conv_b, head_g)


def _l0_out_proj_kernel(ya_ref, yb_ref, x_ref, wa_ref, wb_ref, g_ref, h_ref):
    y = _dot(ya_ref[...], wa_ref[...]) + _dot(yb_ref[...], wb_ref[...])
    h_ref[...] = x_ref[...] + y * _rms_scale(y) * g_ref[...]


def _l0_out_proj(ya, yb, x2, w_out, post_g, *, tm=1024):
    m = x2.shape[0]
    assert DA_WIDTH == ML_WIDTH
    row = lambda width: pl.BlockSpec((tm, width), lambda i: (i, 0))
    w_rows = lambda blk: pl.BlockSpec((DA_WIDTH, D_MODEL), lambda i: (blk, 0))
    return pl.pallas_call(
        _l0_out_proj_kernel,
        grid=(m // tm,),
        in_specs=[row(DA_WIDTH), row(ML_WIDTH), row(D_MODEL),
                  w_rows(0), w_rows(1), pl.BlockSpec((1, D_MODEL), lambda i: (0, 0))],
        out_specs=row(D_MODEL),
        out_shape=jax.ShapeDtypeStruct((m, D_MODEL), F32),
        compiler_params=pltpu.CompilerParams(
            dimension_semantics=("arbitrary",), vmem_limit_bytes=VMEM_LIMIT),
        name="l0_out_proj",
    )(ya, yb, x2, w_out, w_out, post_g)


def _l1_gmlp_kernel(h_ref, pre_g_ref, wu_ref, wv_ref, wz_ref, sg_g_ref, wsp_ref, bsp_ref,
                    wo_ref, post_g_ref, o_ref, vn_ref, y_ref, *, tm):
    hres = h_ref[...]
    hn = (hres * _rms_scale(hres) * pre_g_ref[...]).astype(BF16)

    v = _gelu_tanh(_dot(hn, wv_ref[...]))
    vn_ref[...] = (v * _rms_scale(v) * sg_g_ref[...]).astype(BF16)

    r_i = lax.broadcasted_iota(jnp.int32, (SG_CHUNK, SG_CHUNK), 0)
    c_i = lax.broadcasted_iota(jnp.int32, (SG_CHUNK, SG_CHUNK), 1)
    causal = c_i <= r_i
    for g in range(SG_GROUPS):
        cols = slice(g * SG_GROUP_DIM, (g + 1) * SG_GROUP_DIM)
        u = _gelu_tanh(_dot(hn, wu_ref[:, cols]))
        gate = _silu(_dot(hn, wz_ref[:, cols]))
        wm = jnp.where(causal, wsp_ref[g], 0.0).astype(BF16)
        bias = bsp_ref[g]
        for c in range(tm // SG_CHUNK):
            rows = slice(c * SG_CHUNK, (c + 1) * SG_CHUNK)
            vs = _dot(wm, vn_ref[rows, cols]) + bias
            y_ref[rows, cols] = (u[rows] * vs * gate[rows]).astype(BF16)

    out = _dot(y_ref[...], wo_ref[...])
    o_ref[...] = hres + out * _rms_scale(out) * post_g_ref[...]


def _l1_gmlp(h1, pre_g, w_in, sg_g, w_spatial, b_spatial, wo, post_g, *, tm=1024):
    m = h1.shape[0]
    row = pl.BlockSpec((tm, D_MODEL), lambda i: (i, 0))
    const2 = lambda shape: pl.BlockSpec(shape, lambda i: (0, 0), pipeline_mode=pl.Buffered(1))
    const3 = lambda shape: pl.BlockSpec(shape, lambda i: (0, 0, 0), pipeline_mode=pl.Buffered(1))
    w_cols = lambda blk: pl.BlockSpec((D_MODEL, SG_WIDTH), lambda i: (0, blk), pipeline_mode=pl.Buffered(1))
    return pl.pallas_call(
        functools.partial(_l1_gmlp_kernel, tm=tm),
        grid=(m // tm,),
        in_specs=[
            row, const2((1, D_MODEL)),
            w_cols(0), w_cols(1), w_cols(2),
            const2((1, SG_WIDTH)),
            const3((SG_GROUPS, SG_CHUNK, SG_CHUNK)), const3((SG_GROUPS, SG_CHUNK, 1)),
            const2((SG_WIDTH, D_MODEL)), const2((1, D_MODEL)),
        ],
        out_specs=row,
        out_shape=jax.ShapeDtypeStruct((m, D_MODEL), F32),
        scratch_shapes=[pltpu.VMEM((tm, SG_WIDTH), BF16), pltpu.VMEM((tm, SG_WIDTH), BF16)],
        compiler_params=pltpu.CompilerParams(
            dimension_semantics=("arbitrary",), vmem_limit_bytes=VMEM_LIMIT),
        name="l1_gmlp",
    )(h1, pre_g, w_in, w_in, w_in, sg_g, w_spatial, b_spatial, wo, post_g)


def kernel(x, l0_pre_g, l0_w_in, l0_b_igate, l0_b_fgate, l0_conv_w, l0_conv_b, l0_lambda_q1, l0_lambda_k1, l0_lambda_q2, l0_lambda_k2, l0_da_head_g, l0_ml_head_g, l0_w_out, l0_post_g, l1_pre_g, l1_w_in, l1_sg_norm_g, l1_w_spatial, l1_b_spatial, l1_w_out, l1_post_g):
    b, s, d = x.shape
    m = b * s
    x2 = x.reshape(m, d)
    row = lambda v: v.reshape(1, -1).astype(F32)

    w_in0 = l0_w_in.astype(BF16)
    w_main = jnp.concatenate([w_in0[:, :_OFF_BI], w_in0[:, _OFF_BO:]], axis=1)
    w_gate = jnp.pad(w_in0[:, _OFF_BI:_OFF_BO], ((0, 0), (0, LANES - 2 * ML_HEADS)))
    b_gate = jnp.pad(jnp.concatenate([l0_b_igate, l0_b_fgate]), (0, LANES - 2 * ML_HEADS)).reshape(1, LANES)
    lam_vecs = jnp.stack([l0_lambda_q1, l0_lambda_k1, l0_lambda_q2, l0_lambda_k2]).astype(F32)

    p, gates_t = _l0_in_proj(x2, row(l0_pre_g), w_main, w_gate, b_gate.astype(F32))
    p3 = p.reshape(b, s, L0_MAIN)
    gates_t = gates_t.reshape(2 * ML_HEADS, m // ML_CHUNK, ML_CHUNK)

    y_a = _diff_attn(p3, lam_vecs, row(l0_da_head_g))
    y_b = _mlstm(p3, gates_t, l0_conv_w.astype(F32), row(l0_conv_b), row(l0_ml_head_g))

    h1 = _l0_out_proj(y_a.reshape(m, DA_WIDTH), y_b.reshape(m, ML_WIDTH), x2,
                      l0_w_out.astype(BF16), row(l0_post_g))

    h2 = _l1_gmlp(h1, row(l1_pre_g), l1_w_in.astype(BF16),
                  row(l1_sg_norm_g), l1_w_spatial.astype(F32),
                  l1_b_spatial.astype(F32).reshape(SG_GROUPS, SG_CHUNK, 1),
                  l1_w_out.astype(BF16), row(l1_post_g))
    return h2.reshape(b, s, d)
```
